```python
import math
import jax
import jax.numpy as jnp
from jax import lax
import numpy as np

D_MODEL = 4096
BATCH = 1
SEQ = 8192
DEPTH = 1
DEC_BATCH = 128
DEC_SEQ = 1
PAST_LEN = 2048
PAGE_SIZE = 128

RET_HEADS = 8
RET_DK = D_MODEL // 16
RET_DV = D_MODEL // 16
RET_CHUNK = 128
DIFF_HEADS = 8
DIFF_D = D_MODEL // 32
DIFF_DV = 2 * DIFF_D
Q_BLOCK = 128
PEER_HEADS = 8
PEER_NKEYS = 128
PEER_N = PEER_NKEYS * PEER_NKEYS
PEER_QDIM = 256
PEER_TOPK = 16
PEER_BLOCK = 128
ROPE_THETA = 10000.0
EPS = 1e-6
POOL_NUM = 5
POOL_DEN = 4

RET_QK_W = RET_HEADS * RET_DK
RET_V_W = RET_HEADS * RET_DV
DIFF_QK_W = DIFF_HEADS * 2 * DIFF_D
DIFF_V_W = DIFF_HEADS * DIFF_DV
SPLIT_SIZES = (RET_QK_W, RET_QK_W, RET_V_W, RET_V_W, DIFF_QK_W, DIFF_QK_W, DIFF_V_W, D_MODEL, D_MODEL)
SPLIT_POINTS = tuple(int(p) for p in np.cumsum(SPLIT_SIZES)[:-1])
IN_W = int(sum(SPLIT_SIZES))

kernel_name = "retention_diffattn_peer_hybrid_step"

F32 = jnp.float32


def rmsnorm(x, g=None):
    xf = x.astype(F32)
    y = xf * lax.rsqrt(jnp.mean(xf * xf, axis=-1, keepdims=True) + EPS)
    if g is not None:
        y = y * g.astype(F32)
    return y.astype(x.dtype)


def rope(x, pos):
    d = x.shape[-1]
    inv = ROPE_THETA ** (-jnp.arange(0, d, 2, dtype=F32) / d)
    ang = pos.astype(F32)[:, None] * inv[None, :]
    shape = (1, pos.shape[0]) + (1,) * (x.ndim - 3) + (d // 2,)
    c = jnp.cos(ang).reshape(shape)
    s = jnp.sin(ang).reshape(shape)
    xf = x.astype(F32)
    x1, x2 = xf[..., : d // 2], xf[..., d // 2:]
    return jnp.concatenate([x1 * c - x2 * s, x2 * c + x1 * s], axis=-1).astype(x.dtype)


def ret_log_decay():
    return jnp.log(1.0 - jnp.exp2(-5.0 - jnp.arange(RET_HEADS, dtype=F32)))


def retention_chunk(S, q, k, v):
    log_g = ret_log_decay()
    L = q.shape[1]
    idx = jnp.arange(L, dtype=F32)
    diff = idx[:, None] - idx[None, :]
    decay = jnp.where(diff >= 0, jnp.exp(log_g[:, None, None] * jnp.maximum(diff, 0.0)), 0.0)
    qf, kf, vf = q.astype(F32), k.astype(F32), v.astype(F32)
    scores = jnp.einsum("blhd,bmhd->bhlm", qf, kf) * decay
    o_inner = jnp.einsum("bhlm,bmhe->blhe", scores, vf)
    cross_w = jnp.exp(log_g[None, :] * (idx[:, None] + 1.0))
    o_cross = jnp.einsum("blhd,bhde->blhe", qf, S) * cross_w[None, :, :, None]
    upd_w = jnp.exp(log_g[None, :] * (L - 1.0 - idx[:, None]))
    S_new = jnp.exp(log_g * L)[None, :, None, None] * S + jnp.einsum(
        "blhd,blhe->bhde", kf * upd_w[None, :, :, None], vf)
    return o_inner + o_cross, S_new


def retention_prompt(q, k, v):
    B, T, H, dk = q.shape
    dv = v.shape[-1]
    nc = T // RET_CHUNK

    def to_chunks(a):
        return a.reshape(B, nc, RET_CHUNK, H, a.shape[-1]).swapaxes(0, 1)

    def step(S, qkv):
        o, S_new = retention_chunk(S, *qkv)
        return S_new, o

    S_final, o = lax.scan(step, jnp.zeros((B, H, dk, dv), F32), (to_chunks(q), to_chunks(k), to_chunks(v)))
    return o.swapaxes(0, 1).reshape(B, T, H, dv), S_final


def diff_core(q, k, v, mask, lam):
    s = jnp.einsum("bqhnd,bkhnd->bhnqk", q, k) * (q.shape[-1] ** -0.5)
    p = jax.nn.softmax(jnp.where(mask, s, -jnp.inf), axis=-1)
    a = p[:, :, 0] - lam * p[:, :, 1]
    return jnp.einsum("bhqk,bkhe->bqhe", a, v)


def diff_prompt(q, k, v, lam):
    B, T = q.shape[:2]
    nb = T // Q_BLOCK
    qf = q.astype(F32).reshape((B, nb, Q_BLOCK) + q.shape[2:]).swapaxes(0, 1)
    kf, vf = k.astype(F32), v.astype(F32)
    kpos = jnp.arange(T)

    def block(args):
        qb, start = args
        qpos = start + jnp.arange(Q_BLOCK)
        return diff_core(qb, kf, vf, kpos[None, :] <= qpos[:, None], lam)

    o = lax.map(block, (qf, jnp.arange(nb) * Q_BLOCK))
    return o.swapaxes(0, 1).reshape(B, T, DIFF_HEADS, DIFF_DV)


def diff_sample(q, k_new, v_new, cache_k, cache_v, page_table, lam):
    past = page_table.shape[1] * cache_k.shape[1]
    L = q.shape[1]
    mask = jnp.arange(past + L)[None, :] <= (past + jnp.arange(L))[:, None]

    def one(args):
        qi, ki, vi, pages = args
        kp = cache_k[pages].reshape((past,) + cache_k.shape[2:]).astype(F32)
        vp = cache_v[pages].reshape((past,) + cache_v.shape[2:]).astype(F32)
        kk = jnp.concatenate([kp, ki.astype(F32)], axis=0)
        vv = jnp.concatenate([vp, vi.astype(F32)], axis=0)
        return diff_core(qi.astype(F32)[None], kk[None], vv[None], mask, lam)[0]

    return lax.map(one, (q, k_new, v_new, page_table))


def peer(h, w_q, keys, u_tab, v_tab):
    T, D = h.shape
    nb = -(-T // PEER_BLOCK)
    hp = jnp.pad(h, ((0, nb * PEER_BLOCK - T), (0, 0))).reshape(nb, PEER_BLOCK, D)
    kk = PEER_TOPK * PEER_TOPK

    def block(hb):
        q = (hb @ w_q).astype(F32).reshape(PEER_BLOCK, PEER_HEADS, 2, PEER_QDIM // 2)
        s = jnp.einsum("thnd,hnkd->thnk", q, keys.astype(F32))
        s1, i1 = lax.top_k(s[:, :, 0], PEER_TOPK)
        s2, i2 = lax.top_k(s[:, :, 1], PEER_TOPK)
        cand = (s1[..., :, None] + s2[..., None, :]).reshape(PEER_BLOCK, PEER_HEADS, kk)
        cand_id = (i1[..., :, None] * PEER_NKEYS + i2[..., None, :]).reshape(PEER_BLOCK, PEER_HEADS, kk)
        top, sel = lax.top_k(cand, PEER_TOPK)
        ids = jnp.take_along_axis(cand_id, sel, axis=-1)
        g = jax.nn.softmax(top, axis=-1)
        act = jax.nn.gelu(jnp.einsum("td,thkd->thk", hb, u_tab[ids], preferred_element_type=F32), approximate=False)
        return jnp.einsum("thk,thkd->td", g * act, v_tab[ids]).astype(h.dtype)

    return lax.map(block, hp).reshape(nb * PEER_BLOCK, D)[:T]


def trunk_layer(x, pos, ret_mix, diff_mix, lam_init, norm_mix, w_in, lq1, lk1, lq2, lk2, subln,
                w_ret_out, w_diff_out, w_o, norm_ffn, peer_w_q, peer_keys, peer_u, peer_v):
    B, T, D = x.shape
    xn = rmsnorm(x, norm_mix)
    rq, rk, rv, rg, dq, dk, dv, ga, gb = jnp.split(xn @ w_in, SPLIT_POINTS, axis=-1)
    rq = rope(rq.reshape(B, T, RET_HEADS, RET_DK), pos)
    rk = rope(rk.reshape(B, T, RET_HEADS, RET_DK), pos) * (RET_DK ** -0.5)
    rv = rv.reshape(B, T, RET_HEADS, RET_DV)
    ret_o, ret_state = ret_mix(rq, rk, rv)
    ret_o = (rmsnorm(ret_o) * jax.nn.silu(rg.reshape(B, T, RET_HEADS, RET_DV).astype(F32)))
    ret_o = ret_o.reshape(B, T, RET_V_W).astype(x.dtype)
    dq = rope(dq.reshape(B, T, DIFF_HEADS, 2, DIFF_D), pos)
    dk = rope(dk.reshape(B, T, DIFF_HEADS, 2, DIFF_D), pos)
    dv = dv.reshape(B, T, DIFF_HEADS, DIFF_DV)
    lam = (jnp.exp(jnp.sum(lq1.astype(F32) * lk1.astype(F32)))
           - jnp.exp(jnp.sum(lq2.astype(F32) * lk2.astype(F32))) + lam_init)
    diff_o = diff_mix(dq, dk, dv, lam)
    diff_o = (rmsnorm(diff_o, subln) * (1.0 - lam_init)).reshape(B, T, DIFF_V_W).astype(x.dtype)
    merged = jax.nn.sigmoid(ga) * (ret_o @ w_ret_out) + jax.nn.sigmoid(gb) * (diff_o @ w_diff_out)
    x = x + merged @ w_o
    hn = rmsnorm(x, norm_ffn).reshape(B * T, D)
    x = x + peer(hn, peer_w_q, peer_keys, peer_u, peer_v).reshape(B, T, D)
    return x, ret_state, dk, dv


def setup_inputs(seed: int = 0) -> dict:
    key = jax.random.key(seed)
    ks = jax.random.split(key, 24)
    n_pages = PAST_LEN // PAGE_SIZE
    n_used = DEC_BATCH * n_pages
    n_pool = -(-n_used * POOL_NUM // POOL_DEN)

    def nrm(k, shape, scale):
        return jax.random.normal(k, shape, F32) * scale

    def gain(k, shape):
        return 1.0 + 0.02 * jax.random.normal(k, shape, F32)

    page_table = jax.random.permutation(ks[5], n_pool)[:n_used].reshape(DEC_BATCH, n_pages).astype(jnp.int32)
    return {
        "x_prompt": nrm(ks[0], (BATCH, SEQ, D_MODEL), 1.0),
        "x_sample": nrm(ks[1], (DEC_BATCH, DEC_SEQ, D_MODEL), 1.0),
        "cache_k": nrm(ks[2], (DEPTH, n_pool, PAGE_SIZE, DIFF_HEADS, 2, DIFF_D), 1.0),
        "cache_v": nrm(ks[3], (DEPTH, n_pool, PAGE_SIZE, DIFF_HEADS, DIFF_DV), 1.0),
        "state_ret": nrm(ks[4], (DEPTH, DEC_BATCH, RET_HEADS, RET_DK, RET_DV), 0.5),
        "page_table": page_table,
        "norm_mix": gain(ks[6], (DEPTH, D_MODEL)),
        "w_in": nrm(ks[7], (DEPTH, D_MODEL, IN_W), D_MODEL ** -0.5),
        "lambda_q1": nrm(ks[8], (DEPTH, DIFF_D), 0.1),
        "lambda_k1": nrm(ks[9], (DEPTH, DIFF_D), 0.1),
        "lambda_q2": nrm(ks[10], (DEPTH, DIFF_D), 0.1),
        "lambda_k2": nrm(ks[11], (DEPTH, DIFF_D), 0.1),
        "diff_subln": gain(ks[12], (DEPTH, DIFF_DV)),
        "w_ret_out": nrm(ks[13], (DEPTH, RET_V_W, D_MODEL), RET_V_W ** -0.5),
        "w_diff_out": nrm(ks[14], (DEPTH, DIFF_V_W, D_MODEL), DIFF_V_W ** -0.5),
        "w_o": nrm(ks[15], (DEPTH, D_MODEL, D_MODEL), D_MODEL ** -0.5),
        "norm_ffn": gain(ks[16], (DEPTH, D_MODEL)),
        "peer_w_q": nrm(ks[17], (DEPTH, D_MODEL, PEER_HEADS * PEER_QDIM), D_MODEL ** -0.5),
        "peer_keys": nrm(ks[18], (DEPTH, PEER_HEADS, 2, PEER_NKEYS, PEER_QDIM // 2), (PEER_QDIM // 2) ** -0.5),
        "peer_u": nrm(ks[19], (DEPTH, PEER_N, D_MODEL), D_MODEL ** -0.5),
        "peer_v": nrm(ks[20], (DEPTH, PEER_N, D_MODEL), PEER_HEADS ** -0.5),
        "norm_final": gain(ks[21], (D_MODEL,)),
    }


def reference(x_prompt, x_sample, cache_k, cache_v, state_ret, page_table, norm_mix, w_in,
              lambda_q1, lambda_k1, lambda_q2, lambda_k2, diff_subln, w_ret_out, w_diff_out, w_o,
              norm_ffn, peer_w_q, peer_keys, peer_u, peer_v, norm_final):
    past_len = page_table.shape[1] * cache_k.shape[2]
    pos_p = jnp.arange(x_prompt.shape[1])
    pos_s = past_len + jnp.arange(x_sample.shape[1])
    xp, xs = x_prompt, x_sample
    kp_l, vp_l, sp_l, ks_l, vs_l, ss_l = [], [], [], [], [], []
    for l in range(DEPTH):
        lam_init = 0.8 - 0.6 * math.exp(-0.3 * l)
        common = (norm_mix[l], w_in[l], lambda_q1[l], lambda_k1[l], lambda_q2[l], lambda_k2[l],
                  diff_subln[l], w_ret_out[l], w_diff_out[l], w_o[l], norm_ffn[l],
                  peer_w_q[l], peer_keys[l], peer_u[l], peer_v[l])
        xp, sp, kp, vp = trunk_layer(
            xp, pos_p, retention_prompt, diff_prompt, lam_init, *common)
        ret_state_l = state_ret[l].astype(F32)
        ck, cv = cache_k[l], cache_v[l]
        xs, ss, ks_, vs_ = trunk_layer(
            xs, pos_s,
            lambda q, k, v: retention_chunk(ret_state_l, q, k, v),
            lambda q, k, v, lam: diff_sample(q, k, v, ck, cv, page_table, lam),
            lam_init, *common)
        kp_l.append(kp); vp_l.append(vp); sp_l.append(sp)
        ks_l.append(ks_); vs_l.append(vs_); ss_l.append(ss)
    y_prompt = rmsnorm(xp, norm_final)
    y_sample = rmsnorm(xs, norm_final)
    k_prompt = jnp.stack(kp_l)
    v_prompt = jnp.stack(vp_l)
    state_ret_prompt = jnp.stack(sp_l)
    k_sample = jnp.stack(ks_l)
    v_sample = jnp.stack(vs_l)
    state_ret_sample = jnp.stack(ss_l)
    return (y_prompt, y_sample, k_prompt, v_prompt, state_ret_prompt, k_sample, v_sample, state_ret_sample)
```

```python
import functools
import math

import jax
import jax.numpy as jnp
import numpy as np
from jax import lax
from jax.experimental import pallas as pl
from jax.experimental.pallas import tpu as pltpu

F32 = jnp.float32
BF16 = jnp.bfloat16

LANES = 128
VMEM_LIMIT = 56 * 1024 * 1024

EPS = 1e-6
ROPE_THETA = 10000.0
N_HEADS = 8
HEAD_W = 256
DIFF_D = 128
PEER_NKEYS = 128
PEER_TOPK = 16
RET_L = 256
INV_SQRT2 = 0.7071067811865476

NT_DIMS = (((1,), (1,)), ((), ()))
TN_DIMS = (((0,), (0,)), ((), ()))


def _cparams(sem):
    return pltpu.CompilerParams(dimension_semantics=sem, vmem_limit_bytes=VMEM_LIMIT)


def _nt(a, b):
    return lax.dot_general(a, b, NT_DIMS, preferred_element_type=F32)


def _rms(x):
    return x * lax.rsqrt(jnp.mean(x * x, axis=-1, keepdims=True) + EPS)


def _rmsnorm_body(x_ref, g_ref, o_ref):
    x = x_ref[...]
    o_ref[...] = (_rms(x) * g_ref[...]).astype(o_ref.dtype)


def _rmsnorm(x, g, out_dtype, tm=256):
    m, d = x.shape
    tm = min(tm, m)
    return pl.pallas_call(
        _rmsnorm_body,
        grid=(m // tm,),
        in_specs=[pl.BlockSpec((tm, d), lambda i: (i, 0)), pl.BlockSpec((1, d), lambda i: (0, 0))],
        out_specs=pl.BlockSpec((tm, d), lambda i: (i, 0)),
        out_shape=jax.ShapeDtypeStruct((m, d), out_dtype),
        compiler_params=_cparams(("parallel",)),
        name="rmsnorm",
    )(x, g.reshape(1, d))


def _add_rmsnorm_body(x_ref, y_ref, g_ref, o_ref):
    o_ref[...] = (_rms(x_ref[...] + y_ref[...]) * g_ref[...]).astype(o_ref.dtype)


def _add_rmsnorm(x, y, g, tm=256):
    m, d = x.shape
    tm = min(tm, m)
    row = pl.BlockSpec((tm, d), lambda i: (i, 0))
    return pl.pallas_call(
        _add_rmsnorm_body,
        grid=(m // tm,),
        in_specs=[row, row, pl.BlockSpec((1, d), lambda i: (0, 0))],
        out_specs=row,
        out_shape=jax.ShapeDtypeStruct((m, d), F32),
        compiler_params=_cparams(("parallel",)),
        name="add_rmsnorm",
    )(x, y, g.reshape(1, d))


def _proj_body(x_ref, w_ref, *rest, epi, scale, n_extra, tn):
    extras, outs = rest[:n_extra], rest[n_extra:]
    acc = jnp.dot(x_ref[...], w_ref[...], preferred_element_type=F32)
    if epi == "plain":
        res = acc
    elif epi == "sigmoid":
        res = jax.nn.sigmoid(acc)
    elif epi == "residual":
        res = acc + extras[0][...]
    elif epi == "rope256":
        c, s = extras[0][...], extras[1][...]
        parts = []
        for b in range(tn // HEAD_W):
            x1 = acc[:, b * HEAD_W:b * HEAD_W + LANES]
            x2 = acc[:, b * HEAD_W + LANES:(b + 1) * HEAD_W]
            parts += [x1 * c - x2 * s, x2 * c + x1 * s]
        res = jnp.concatenate(parts, axis=1) * scale
    elif epi == "rope128":
        cc, ss = extras[0][...], extras[1][...]
        parts = []
        for b in range(tn // LANES):
            xb = acc[:, b * LANES:(b + 1) * LANES]
            parts.append(xb * cc + pltpu.roll(xb, LANES // 2, 1) * ss)
        res = jnp.concatenate(parts, axis=1) * scale
    else:
        raise ValueError(epi)
    for o in outs:
        o[...] = res.astype(o.dtype)


def _proj(x, w, col0, ncols, epi, out_dtypes, extras=(), scale=1.0, tm=1024, tn=512, name="proj"):
    m, k = x.shape
    tm = min(tm, m)
    assert m % tm == 0 and ncols % tn == 0 and col0 % tn == 0
    j0 = col0 // tn
    in_specs = [pl.BlockSpec((tm, k), lambda i, j: (i, 0)),
                pl.BlockSpec((k, tn), lambda i, j: (0, j + j0))]
    args = [x, w]
    for arr, kind in extras:
        if kind == "row":
            in_specs.append(pl.BlockSpec((tm, arr.shape[1]), lambda i, j: (i, 0)))
        else:
            in_specs.append(pl.BlockSpec((tm, tn), lambda i, j: (i, j)))
        args.append(arr)
    out_spec = pl.BlockSpec((tm, tn), lambda i, j: (i, j))
    outs = pl.pallas_call(
        functools.partial(_proj_body, epi=epi, scale=scale, n_extra=len(extras), tn=tn),
        grid=(m // tm, ncols // tn),
        in_specs=in_specs,
        out_specs=[out_spec] * len(out_dtypes),
        out_shape=[jax.ShapeDtypeStruct((m, ncols), dt) for dt in out_dtypes],
        compiler_params=_cparams(("parallel", "arbitrary")),
        name=name,
    )(*args)
    return outs


def _merge_body(r_ref, d_ref, wr_ref, wd_ref, ga_ref, gb_ref, o_ref):
    a = jnp.dot(r_ref[...], wr_ref[...], preferred_element_type=F32)
    b = jnp.dot(d_ref[...], wd_ref[...], preferred_element_type=F32)
    o_ref[...] = (ga_ref[...].astype(F32) * a + gb_ref[...].astype(F32) * b).astype(o_ref.dtype)


def _merge(ret_o, diff_o, w_ret, w_diff, sig_a, sig_b, tm=1024, tn=512):
    m, k = ret_o.shape
    n = w_ret.shape[1]
    tm = min(tm, m)
    row = pl.BlockSpec((tm, k), lambda i, j: (i, 0))
    wcol = pl.BlockSpec((k, tn), lambda i, j: (0, j))
    tile = pl.BlockSpec((tm, tn), lambda i, j: (i, j))
    return pl.pallas_call(
        _merge_body,
        grid=(m // tm, n // tn),
        in_specs=[row, row, wcol, wcol, tile, tile],
        out_specs=tile,
        out_shape=jax.ShapeDtypeStruct((m, n), BF16),
        compiler_params=_cparams(("parallel", "arbitrary")),
        name="merge",
    )(ret_o, diff_o, w_ret, w_diff, sig_a, sig_b)


def _ret_log_decay():
    return jnp.log(1.0 - jnp.exp2(-5.0 - jnp.arange(N_HEADS, dtype=F32)))


def _silu(g):
    return g * jax.nn.sigmoid(g)


def _ret_prompt_body(gl_ref, q_ref, k_ref, v_ref, g_ref, dec_ref, cw_ref, uw_ref, o_ref, st_ref, s_sc):
    c = pl.program_id(0)

    @pl.when(c == 0)
    def _():
        s_sc[...] = jnp.zeros_like(s_sc)

    for h in range(N_HEADS):
        sl = slice(h * HEAD_W, (h + 1) * HEAD_W)
        q, k, v = q_ref[:, sl], k_ref[:, sl], v_ref[:, sl]
        state = s_sc[h]
        scores = _nt(q, k) * dec_ref[h]
        o = jnp.dot(scores.astype(BF16), v, preferred_element_type=F32)
        o = o + jnp.dot(q, state.astype(BF16), preferred_element_type=F32) * cw_ref[h]
        kw = (k.astype(F32) * uw_ref[h]).astype(BF16)
        s_sc[h] = gl_ref[h] * state + lax.dot_general(kw, v, TN_DIMS, preferred_element_type=F32)
        o_ref[:, sl] = (_rms(o) * _silu(g_ref[:, sl].astype(F32))).astype(o_ref.dtype)

    @pl.when(c == pl.num_programs(0) - 1)
    def _():
        st_ref[...] = s_sc[...]


def _ret_prompt(rq, rk, rv, rg):
    t = rq.shape[0]
    L = RET_L
    log_g = _ret_log_decay()
    idx = jnp.arange(L, dtype=F32)
    diff = idx[:, None] - idx[None, :]
    decay = jnp.where(diff >= 0, jnp.exp(log_g[:, None, None] * jnp.maximum(diff, 0.0)), 0.0)
    cw = jnp.exp(log_g[:, None] * (idx[None, :] + 1.0))
    uw = jnp.exp(log_g[:, None] * (L - 1.0 - idx[None, :]))
    cw = jnp.broadcast_to(cw[:, :, None], (N_HEADS, L, HEAD_W))
    uw = jnp.broadcast_to(uw[:, :, None], (N_HEADS, L, HEAD_W))
    gl = jnp.exp(log_g * L)
    rows = pl.BlockSpec((L, N_HEADS * HEAD_W), lambda c: (c, 0))
    const3 = lambda shape: pl.BlockSpec(shape, lambda c: (0, 0, 0))
    o, st = pl.pallas_call(
        _ret_prompt_body,
        grid=(t // L,),
        in_specs=[pl.BlockSpec(memory_space=pltpu.SMEM), rows, rows, rows, rows,
                  const3((N_HEADS, L, L)), const3((N_HEADS, L, HEAD_W)), const3((N_HEADS, L, HEAD_W))],
        out_specs=[rows, const3((N_HEADS, HEAD_W, HEAD_W))],
        out_shape=[jax.ShapeDtypeStruct((t, N_HEADS * HEAD_W), BF16),
                   jax.ShapeDtypeStruct((N_HEADS, HEAD_W, HEAD_W), F32)],
        scratch_shapes=[pltpu.VMEM((N_HEADS, HEAD_W, HEAD_W), F32)],
        compiler_params=_cparams(("arbitrary",)),
        name="ret_prompt",
    )(gl, rq, rk, rv, rg, decay, cw, uw)
    return o, st


def _ret_sample_body(gam_ref, q_ref, k_ref, v_ref, g_ref, s_ref, o_ref, sn_ref):
    for h in range(N_HEADS):
        state = s_ref[0, h]
        q, k = q_ref[0, h], k_ref[0, h]
        v = v_ref[0, h]
        gam = gam_ref[h]
        qk = jnp.sum(q * k, axis=0, keepdims=True)
        o = qk * v + gam * jnp.sum(q * state, axis=0, keepdims=True)
        sn_ref[0, h] = gam * state + k * v
        o_ref[0, h] = (_rms(o) * _silu(g_ref[0, h])).astype(o_ref.dtype)


def _ret_sample(rq, rk, rv, rg, state):
    b = rq.shape[0]
    gam = jnp.exp(_ret_log_decay())
    col = lambda a: a.reshape(b, N_HEADS, HEAD_W, 1)
    rowv = lambda a: a.reshape(b, N_HEADS, 1, HEAD_W)
    col_spec = pl.BlockSpec((1, N_HEADS, HEAD_W, 1), lambda i: (i, 0, 0, 0))
    row_spec = pl.BlockSpec((1, N_HEADS, 1, HEAD_W), lambda i: (i, 0, 0, 0))
    st_spec = pl.BlockSpec((1, N_HEADS, HEAD_W, HEAD_W), lambda i: (i, 0, 0, 0))
    o, sn = pl.pallas_call(
        _ret_sample_body,
        grid=(b,),
        in_specs=[pl.BlockSpec(memory_space=pltpu.SMEM), col_spec, col_spec, row_spec, row_spec, st_spec],
        out_specs=[row_spec, st_spec],
        out_shape=[jax.ShapeDtypeStruct((b, N_HEADS, 1, HEAD_W), BF16),
                   jax.ShapeDtypeStruct(state.shape, F32)],
        compiler_params=_cparams(("parallel",)),
        name="ret_sample",
    )(gam, col(rq), col(rk), rowv(rv), rowv(rg), state)
    return o.reshape(b, N_HEADS * HEAD_W), sn


def _lambda(lv_ref, lam_init):
    lv = lv_ref[...]
    e1 = jnp.exp(jnp.sum(lv[0:1] * lv[1:2], axis=-1, keepdims=True))
    e2 = jnp.exp(jnp.sum(lv[2:3] * lv[3:4], axis=-1, keepdims=True))
    return e1 - e2 + lam_init


def _diff_prompt_body(q_ref, k_ref, v_ref, lv_ref, g_ref, o_ref, acc1, acc2, *, tq, lam_init):
    qi = pl.program_id(1)
    q1, q2 = q_ref[:, :DIFF_D], q_ref[:, DIFF_D:]

    def update(s, m, l, acc, v):
        m_new = jnp.maximum(m, jnp.max(s, axis=-1, keepdims=True))
        alpha = jnp.exp(m - m_new)
        p = jnp.exp(s - m_new)
        l_new = alpha * l + jnp.sum(p, axis=-1, keepdims=True)
        acc[...] = alpha * acc[...] + jnp.dot(p.astype(BF16), v, preferred_element_type=F32)
        return m_new, l_new

    def step(kv, carry, masked):
        m1, l1, m2, l2 = carry
        ks = pl.multiple_of(kv * tq, tq)
        k = k_ref[pl.ds(ks, tq), :]
        v = v_ref[pl.ds(ks, tq), :]
        s1 = _nt(q1, k[:, :DIFF_D])
        s2 = _nt(q2, k[:, DIFF_D:])
        if masked:
            keep = (lax.broadcasted_iota(jnp.int32, (tq, tq), 1) <= lax.broadcasted_iota(jnp.int32, (tq, tq), 0))
            s1 = jnp.where(keep, s1, -jnp.inf)
            s2 = jnp.where(keep, s2, -jnp.inf)
        m1, l1 = update(s1, m1, l1, acc1, v)
        m2, l2 = update(s2, m2, l2, acc2, v)
        return m1, l1, m2, l2

    acc1[...] = jnp.zeros_like(acc1)
    acc2[...] = jnp.zeros_like(acc2)
    neg = jnp.full((tq, 1), -jnp.inf, F32)
    zero = jnp.zeros((tq, 1), F32)
    carry = lax.fori_loop(0, qi, lambda kv, c: step(kv, c, False), (neg, zero, neg, zero))
    m1, l1, m2, l2 = step(qi, carry, True)
    lam = _lambda(lv_ref, lam_init)
    o = acc1[...] / l1 - lam * (acc2[...] / l2)
    o_ref[...] = (_rms(o) * g_ref[...] * (1.0 - lam_init)).astype(o_ref.dtype)


def _diff_prompt(dq, dk, dv, lam_vecs, subln, lam_init, tq=512):
    t = dq.shape[0]
    tq = min(tq, t)
    qspec = pl.BlockSpec((tq, HEAD_W), lambda h, i: (i, h))
    kvspec = pl.BlockSpec((t, HEAD_W), lambda h, i: (0, h))
    return pl.pallas_call(
        functools.partial(_diff_prompt_body, tq=tq, lam_init=lam_init),
        grid=(N_HEADS, t // tq),
        in_specs=[qspec, kvspec, kvspec,
                  pl.BlockSpec((4, DIFF_D), lambda h, i: (0, 0)),
                  pl.BlockSpec((1, HEAD_W), lambda h, i: (0, 0))],
        out_specs=qspec,
        out_shape=jax.ShapeDtypeStruct((t, N_HEADS * HEAD_W), BF16),
        scratch_shapes=[pltpu.VMEM((tq, HEAD_W), F32), pltpu.VMEM((tq, HEAD_W), F32)],
        compiler_params=_cparams(("parallel", "arbitrary")),
        name="diff_prompt",
    )(dq, dk, dv, lam_vecs, subln.reshape(1, HEAD_W))


def _diff_sample_body(pt_ref, q_ref, kn_ref, vn_ref, ck_ref, cv_ref, lv_ref, g_ref, o_ref,
                      m_sc, l_sc, acc_sc, *, page, lam_init):
    p = pl.program_id(1)

    @pl.when(p == 0)
    def _():
        m_sc[...] = jnp.full_like(m_sc, -jnp.inf)
        l_sc[...] = jnp.zeros_like(l_sc)
        acc_sc[...] = jnp.zeros_like(acc_sc)

    def fold(n, s, v):
        m_old = m_sc[n][:, :1]
        m_new = jnp.maximum(m_old, jnp.max(s, axis=0))
        alpha = jnp.exp(m_old - m_new)
        pr = jnp.exp(s - m_new[None])
        l_new = alpha * l_sc[n][:, :1] + jnp.sum(pr, axis=0)
        acc_sc[n] = alpha * acc_sc[n] + jnp.sum(pr * v, axis=0)
        m_sc[n] = jnp.broadcast_to(m_new, m_sc.shape[1:])
        l_sc[n] = jnp.broadcast_to(l_new, l_sc.shape[1:])

    v3 = cv_ref[0].reshape(page, N_HEADS, HEAD_W)
    for n in range(2):
        kn = ck_ref[0, pl.ds(n, page * N_HEADS, stride=2), :].reshape(page, N_HEADS, DIFF_D)
        s = jnp.sum(kn * q_ref[0, n][None], axis=-1, keepdims=True)
        fold(n, s, v3)

    @pl.when(p == pl.num_programs(1) - 1)
    def _():
        outs = []
        for n in range(2):
            s = jnp.sum(kn_ref[0, n] * q_ref[0, n], axis=-1, keepdims=True)
            fold(n, s[None], vn_ref[...])
            outs.append(acc_sc[n] / l_sc[n][:, :1])
        o = outs[0] - _lambda(lv_ref, lam_init) * outs[1]
        o_ref[0] = (_rms(o) * g_ref[...] * (1.0 - lam_init)).astype(o_ref.dtype)


def _diff_sample(dq, dk, dv, cache_k, cache_v, page_table, lam_vecs, subln, lam_init):
    b, npg = page_table.shape
    n_pool, page = cache_k.shape[:2]
    to_maps = lambda a: a.reshape(b, N_HEADS, 2, DIFF_D).transpose(0, 2, 1, 3)
    ck = cache_k.reshape(n_pool, page * N_HEADS * 2, DIFF_D)
    cv = cache_v.reshape(n_pool, page * N_HEADS, HEAD_W)
    qk_spec = pl.BlockSpec((1, 2, N_HEADS, DIFF_D), lambda i, p, pt: (i, 0, 0, 0))
    hv_spec = pl.BlockSpec((1, N_HEADS, HEAD_W), lambda i, p, pt: (i, 0, 0))
    grid_spec = pltpu.PrefetchScalarGridSpec(
        num_scalar_prefetch=1,
        grid=(b, npg),
        in_specs=[qk_spec, qk_spec, hv_spec,
                  pl.BlockSpec((1, page * N_HEADS * 2, DIFF_D), lambda i, p, pt: (pt[i * npg + p], 0, 0)),
                  pl.BlockSpec((1, page * N_HEADS, HEAD_W), lambda i, p, pt: (pt[i * npg + p], 0, 0)),
                  pl.BlockSpec((4, DIFF_D), lambda i, p, pt: (0, 0)),
                  pl.BlockSpec((1, HEAD_W), lambda i, p, pt: (0, 0))],
        out_specs=hv_spec,
        scratch_shapes=[pltpu.VMEM((2, N_HEADS, LANES), F32), pltpu.VMEM((2, N_HEADS, LANES), F32),
                        pltpu.VMEM((2, N_HEADS, HEAD_W), F32)],
    )
    o = pl.pallas_call(
        functools.partial(_diff_sample_body, page=page, lam_init=lam_init),
        grid_spec=grid_spec,
        out_shape=jax.ShapeDtypeStruct((b, N_HEADS, HEAD_W), BF16),
        compiler_params=_cparams(("parallel", "arbitrary")),
        name="diff_sample",
    )(page_table.reshape(-1), to_maps(dq), to_maps(dk), dv.reshape(b, N_HEADS, HEAD_W), ck, cv,
      lam_vecs, subln.reshape(1, HEAD_W))
    return o.reshape(b, N_HEADS * HEAD_W)


W_PITCH = PEER_NKEYS + 8


def _topk_rows(x, k):
    n = x.shape[0]
    iota = lax.broadcasted_iota(jnp.int32, x.shape, 0).astype(F32)
    vals, idxs = [], []
    for _ in range(k):
        m = jnp.max(x, axis=0, keepdims=True)
        idx = jnp.min(jnp.where(x == m, iota, float(n)), axis=0, keepdims=True)
        x = jnp.where(iota == idx, -jnp.inf, x)
        vals.append(m)
        idxs.append(idx)
    return jnp.concatenate(vals, axis=0), jnp.concatenate(idxs, axis=0)


def _pick_rows(table, sel):
    out = jnp.zeros(sel.shape, F32)
    for r in range(table.shape[0]):
        out = jnp.where(sel == float(r), table[r:r + 1], out)
    return out


def _peer_route_body(q_ref, keys_ref, w_ref, a_sc, b_sc, g_sc, wbuf, *, tm):
    qb = q_ref[...].astype(BF16)
    a_rows, b_rows, g_rows = [], [], []
    for h in range(N_HEADS):
        halves = []
        for n in range(2):
            hn = 2 * h + n
            s_t = _nt(keys_ref[hn], qb[:, hn * LANES:(hn + 1) * LANES])
            halves.append(_topk_rows(s_t, PEER_TOPK))
        (s1, i1), (s2, i2) = halves
        cand = jnp.concatenate([s1[i:i + 1] + s2 for i in range(PEER_TOPK)], axis=0)
        top, sel = _topk_rows(cand, PEER_TOPK)
        sel_i = jnp.floor(sel * (1.0 / PEER_TOPK))
        sel_j = sel - sel_i * PEER_TOPK
        a_rows.append(_pick_rows(i1, sel_i))
        b_rows.append(_pick_rows(i2, sel_j))
        e = jnp.exp(top - top[0:1])
        g_rows.append(e / jnp.sum(e, axis=0, keepdims=True))
    a_sc[...] = jnp.concatenate(a_rows, axis=0).T
    b_sc[...] = jnp.concatenate(b_rows, axis=0).T
    g_sc[...] = jnp.concatenate(g_rows, axis=0).T

    sub = lax.broadcasted_iota(jnp.int32, (PEER_NKEYS, N_HEADS * PEER_TOPK), 0).astype(F32)

    def token(t, carry):
        ar, br, gr = a_sc[pl.ds(t, 1), :], b_sc[pl.ds(t, 1), :], g_sc[pl.ds(t, 1), :]
        p_mat = jnp.where(sub == ar, 1.0, 0.0).astype(BF16)
        q_mat = jnp.where(sub == br, gr, 0.0).astype(BF16)
        wbuf[pl.ds(pl.multiple_of(t * W_PITCH, 8), PEER_NKEYS), :] = _nt(p_mat, q_mat)
        return carry

    lax.fori_loop(0, tm, token, 0)
    for a in range(PEER_NKEYS):
        w_ref[a] = wbuf[pl.ds(a, tm, stride=W_PITCH), :].astype(w_ref.dtype)


def _peer_route(q, keys, tm=128):
    m = q.shape[0]
    tm = min(tm, m)
    slots = N_HEADS * PEER_TOPK
    return pl.pallas_call(
        functools.partial(_peer_route_body, tm=tm),
        grid=(m // tm,),
        in_specs=[pl.BlockSpec((tm, q.shape[1]), lambda i: (i, 0)),
                  pl.BlockSpec(keys.shape, lambda i: (0, 0, 0))],
        out_specs=pl.BlockSpec((PEER_NKEYS, tm, PEER_NKEYS), lambda i: (0, i, 0)),
        out_shape=jax.ShapeDtypeStruct((PEER_NKEYS, m, PEER_NKEYS), BF16),
        scratch_shapes=[pltpu.VMEM((tm, slots), F32), pltpu.VMEM((tm, slots), F32), pltpu.VMEM((tm, slots), F32),
                        pltpu.VMEM((tm * W_PITCH, PEER_NKEYS), F32)],
        compiler_params=_cparams(("parallel",)),
        name="peer_route",
    )(q, keys)


def _peer_dense_body(h_ref, u_ref, v_ref, w_ref, o_ref, *, ta):
    n = pl.program_id(1)
    s = _nt(h_ref[...], u_ref[...])
    act = 0.5 * s * (1.0 + lax.erf(s * INV_SQRT2))
    c = jnp.concatenate([act[:, a * LANES:(a + 1) * LANES] * w_ref[a].astype(F32) for a in range(ta)], axis=1)
    upd = jnp.dot(c.astype(BF16), v_ref[...], preferred_element_type=F32)

    @pl.when(n == 0)
    def _():
        o_ref[...] = upd

    @pl.when(n > 0)
    def _():
        o_ref[...] += upd


def _peer_dense(hn, u, v, w, tm=512, tn=256):
    m, d = hn.shape
    n_exp = u.shape[0]
    tm = min(tm, m)
    ta = tn // LANES
    return pl.pallas_call(
        functools.partial(_peer_dense_body, ta=ta),
        grid=(m // tm, n_exp // tn),
        in_specs=[pl.BlockSpec((tm, d), lambda i, n: (i, 0)),
                  pl.BlockSpec((tn, d), lambda i, n: (n, 0)),
                  pl.BlockSpec((tn, d), lambda i, n: (n, 0)),
                  pl.BlockSpec((ta, tm, LANES), lambda i, n: (n, i, 0))],
        out_specs=pl.BlockSpec((tm, d), lambda i, n: (i, 0)),
        out_shape=jax.ShapeDtypeStruct((m, d), F32),
        compiler_params=_cparams(("parallel", "arbitrary")),
        name="peer_dense",
    )(hn, u, v, w)


def _rope_tables(pos, d):
    inv = ROPE_THETA ** (-jnp.arange(0, d, 2, dtype=F32) / d)
    ang = pos.astype(F32)[:, None] * inv[None, :]
    return jnp.cos(ang), jnp.sin(ang)


def _layer(x, pos, is_prompt, wts, lam_init, mixer_state):
    (norm_mix, w_in, lam_vecs, subln, w_ret_out, w_diff_out, w_o, norm_ffn, peer_w_q, peer_keys,
     peer_u, peer_v) = wts
    m, d = x.shape
    seg = N_HEADS * HEAD_W
    act_dt = BF16 if is_prompt else F32
    xn = _rmsnorm(x, norm_mix, BF16)
    c256, s256 = _rope_tables(pos, HEAD_W)
    c128, s128 = _rope_tables(pos, DIFF_D)
    cc, ss = jnp.concatenate([c128, c128], axis=1), jnp.concatenate([-s128, s128], axis=1)
    rope_r = ((c256, "row"), (s256, "row"))
    rope_d = ((cc, "row"), (ss, "row"))

    def seg_proj(i, n, epi, dts, **kw):
        return _proj(xn, w_in, i * seg, n * seg, epi, dts, name=f"in_proj_{i}", **kw)

    (rq,) = seg_proj(0, 1, "rope256", [act_dt], extras=rope_r)
    (rk,) = seg_proj(1, 1, "rope256", [act_dt], extras=rope_r, scale=HEAD_W ** -0.5)
    (rvg,) = seg_proj(2, 2, "plain", [act_dt])
    (dq,) = seg_proj(4, 1, "rope128", [act_dt], extras=rope_d, scale=DIFF_D ** -0.5)
    dk_f, dk_b = seg_proj(5, 1, "rope128", [F32, BF16], extras=rope_d)
    dv_f, dv_b = seg_proj(6, 1, "plain", [F32, BF16])
    (gates,) = seg_proj(7, 4, "sigmoid", [BF16])
    rv, rg = rvg[:, :seg], rvg[:, seg:]

    if is_prompt:
        ret_o, ret_state = _ret_prompt(rq, rk, rv, rg)
        diff_o = _diff_prompt(dq, dk_b, dv_b, lam_vecs, subln, lam_init)
    else:
        state, cache_k, cache_v, page_table = mixer_state
        ret_o, ret_state = _ret_sample(rq, rk, rv, rg, state)
        diff_o = _diff_sample(dq, dk_f, dv_f, cache_k, cache_v, page_table, lam_vecs, subln, lam_init)

    merged = _merge(ret_o, diff_o, w_ret_out, w_diff_out, gates[:, :d], gates[:, d:])
    (x1,) = _proj(merged, w_o, 0, d, "residual", [F32], extras=((x, "tile"),), name="out_proj")
    hn = _rmsnorm(x1, norm_ffn, BF16)
    (pq,) = _proj(hn, peer_w_q, 0, peer_w_q.shape[1], "plain", [F32], name="peer_q")
    w = _peer_route(pq, peer_keys)
    y = _peer_dense(hn, peer_u, peer_v, w)
    return x1, y, ret_state, dk_f, dv_f


def kernel(x_prompt, x_sample, cache_k, cache_v, state_ret, page_table, norm_mix, w_in, lambda_q1, lambda_k1,
           lambda_q2, lambda_k2, diff_subln, w_ret_out, w_diff_out, w_o, norm_ffn, peer_w_q, peer_keys, peer_u,
           peer_v, norm_final):
    depth = w_in.shape[0]
    assert depth == 1
    b_p, t_p, d = x_prompt.shape
    b_s, t_s, _ = x_sample.shape
    assert b_p == 1 and t_s == 1
    past = page_table.shape[1] * cache_k.shape[2]
    l = 0
    lam_init = 0.8 - 0.6 * math.exp(-0.3 * l)
    lam_vecs = jnp.stack([lambda_q1[l], lambda_k1[l], lambda_q2[l], lambda_k2[l]])
    keys = peer_keys[l].reshape(N_HEADS * 2, PEER_NKEYS, -1).astype(BF16)
    wts = (norm_mix[l], w_in[l].astype(BF16), lam_vecs, diff_subln[l], w_ret_out[l].astype(BF16),
           w_diff_out[l].astype(BF16), w_o[l].astype(BF16), norm_ffn[l], peer_w_q[l].astype(BF16), keys,
           peer_u[l].astype(BF16), peer_v[l].astype(BF16))

    xp = x_prompt.reshape(t_p, d)
    xs = x_sample.reshape(b_s, d)
    x1p, yp, sp, kp, vp = _layer(xp, jnp.arange(t_p), True, wts, lam_init, None)
    x1s, ys, ss, ks, vs = _layer(xs, jnp.full((b_s,), past), False, wts, lam_init,
                                 (state_ret[l], cache_k[l], cache_v[l], page_table))
    y_prompt = _add_rmsnorm(x1p, yp, norm_final).reshape(b_p, t_p, d)
    y_sample = _add_rmsnorm(x1s, ys, norm_final).reshape(b_s, t_s, d)
    k_prompt = kp.reshape(1, b_p, t_p, N_HEADS, 2, DIFF_D)
    v_prompt = vp.reshape(1, b_p, t_p, N_HEADS, HEAD_W)
    k_sample = ks.reshape(1, b_s, t_s, N_HEADS, 2, DIFF_D)
    v_sample = vs.reshape(1, b_s, t_s, N_HEADS, HEAD_W)
    return (y_prompt, y_sample, k_prompt, v_prompt, sp[None, None], k_sample, v_sample, ss[None])
```

```python
import functools
import math

import jax
import jax.numpy as jnp
import numpy as np
from jax import lax
from jax.experimental import pallas as pl
from jax.experimental.pallas import tpu as pltpu

F32 = jnp.float32
BF16 = jnp.bfloat16

LANES = 128
VMEM_LIMIT = 56 * 1024 * 1024

EPS = 1e-6
ROPE_THETA = 10000.0
N_HEADS = 8
HEAD_W = 256
DIFF_D = 128
PEER_NKEYS = 128
PEER_TOPK = 16
RET_L = 256
INV_SQRT2 = 0.7071067811865476

NT_DIMS = (((1,), (1,)), ((), ()))
TN_DIMS = (((0,), (0,)), ((), ()))


def _cparams(sem):
    return pltpu.CompilerParams(dimension_semantics=sem, vmem_limit_bytes=VMEM_LIMIT)


def _nt(a, b):
    return lax.dot_general(a, b, NT_DIMS, preferred_element_type=F32)


def _rms(x):
    return x * lax.rsqrt(jnp.mean(x * x, axis=-1, keepdims=True) + EPS)


def _rmsnorm_body(x_ref, g_ref, o_ref):
    x = x_ref[...]
    o_ref[...] = (_rms(x) * g_ref[...]).astype(o_ref.dtype)


def _rmsnorm(x, g, out_dtype, tm=256):
    m, d = x.shape
    tm = min(tm, m)
    return pl.pallas_call(
        _rmsnorm_body,
        grid=(m // tm,),
        in_specs=[pl.BlockSpec((tm, d), lambda i: (i, 0)), pl.BlockSpec((1, d), lambda i: (0, 0))],
        out_specs=pl.BlockSpec((tm, d), lambda i: (i, 0)),
        out_shape=jax.ShapeDtypeStruct((m, d), out_dtype),
        compiler_params=_cparams(("parallel",)),
        name="rmsnorm",
    )(x, g.reshape(1, d))


def _add_rmsnorm_body(x_ref, y_ref, g_ref, o_ref):
    o_ref[...] = (_rms(x_ref[...] + y_ref[...]) * g_ref[...]).astype(o_ref.dtype)


def _add_rmsnorm(x, y, g, tm=256):
    m, d = x.shape
    tm = min(tm, m)
    row = pl.BlockSpec((tm, d), lambda i: (i, 0))
    return pl.pallas_call(
        _add_rmsnorm_body,
        grid=(m // tm,),
        in_specs=[row, row, pl.BlockSpec((1, d), lambda i: (0, 0))],
        out_specs=row,
        out_shape=jax.ShapeDtypeStruct((m, d), F32),
        compiler_params=_cparams(("parallel",)),
        name="add_rmsnorm",
    )(x, y, g.reshape(1, d))


def _proj_body(x_ref, w_ref, *rest, epi, scale, n_extra, tn):
    extras, outs = rest[:n_extra], rest[n_extra:]
    acc = jnp.dot(x_ref[...], w_ref[...].astype(BF16), preferred_element_type=F32)
    if epi == "plain":
        res = acc
    elif epi == "sigmoid":
        res = jax.nn.sigmoid(acc)
    elif epi == "residual":
        res = acc + extras[0][...]
    elif epi == "rope256":
        c, s = extras[0][...], extras[1][...]
        parts = []
        for b in range(tn // HEAD_W):
            x1 = acc[:, b * HEAD_W:b * HEAD_W + LANES]
            x2 = acc[:, b * HEAD_W + LANES:(b + 1) * HEAD_W]
            parts += [x1 * c - x2 * s, x2 * c + x1 * s]
        res = jnp.concatenate(parts, axis=1) * scale
    elif epi == "rope128":
        cc, ss = extras[0][...], extras[1][...]
        parts = []
        for b in range(tn // LANES):
            xb = acc[:, b * LANES:(b + 1) * LANES]
            parts.append(xb * cc + pltpu.roll(xb, LANES // 2, 1) * ss)
        res = jnp.concatenate(parts, axis=1) * scale
    else:
        raise ValueError(epi)
    for o in outs:
        o[...] = res.astype(o.dtype)


def _proj(x, w, col0, ncols, epi, out_dtypes, extras=(), scale=1.0, tm=1024, tn=512, name="proj"):
    m, k = x.shape
    tm = min(tm, m)
    assert m % tm == 0 and ncols % tn == 0 and col0 % tn == 0
    j0 = col0 // tn
    in_specs = [pl.BlockSpec((tm, k), lambda i, j: (i, 0)),
                pl.BlockSpec((k, tn), lambda i, j: (0, j + j0))]
    args = [x, w]
    for arr, kind in extras:
        if kind == "row":
            in_specs.append(pl.BlockSpec((tm, arr.shape[1]), lambda i, j: (i, 0)))
        else:
            in_specs.append(pl.BlockSpec((tm, tn), lambda i, j: (i, j)))
        args.append(arr)
    out_spec = pl.BlockSpec((tm, tn), lambda i, j: (i, j))
    outs = pl.pallas_call(
        functools.partial(_proj_body, epi=epi, scale=scale, n_extra=len(extras), tn=tn),
        grid=(m // tm, ncols // tn),
        in_specs=in_specs,
        out_specs=[out_spec] * len(out_dtypes),
        out_shape=[jax.ShapeDtypeStruct((m, ncols), dt) for dt in out_dtypes],
        compiler_params=_cparams(("parallel", "arbitrary")),
        name=name,
    )(*args)
    return outs


def _merge_body(r_ref, d_ref, wr_ref, wd_ref, ga_ref, gb_ref, o_ref):
    a = jnp.dot(r_ref[...], wr_ref[...].astype(BF16), preferred_element_type=F32)
    b = jnp.dot(d_ref[...], wd_ref[...].astype(BF16), preferred_element_type=F32)
    o_ref[...] = (ga_ref[...].astype(F32) * a + gb_ref[...].astype(F32) * b).astype(o_ref.dtype)


def _merge(ret_o, diff_o, w_ret, w_diff, gates, tm=1024, tn=512):
    m, k = ret_o.shape
    n = w_ret.shape[1]
    tm = min(tm, m)
    nj = n // tn
    row = pl.BlockSpec((tm, k), lambda i, j: (i, 0))
    wcol = pl.BlockSpec((k, tn), lambda i, j: (0, j))
    tile = pl.BlockSpec((tm, tn), lambda i, j: (i, j))
    return pl.pallas_call(
        _merge_body,
        grid=(m // tm, nj),
        in_specs=[row, row, wcol, wcol, tile, pl.BlockSpec((tm, tn), lambda i, j: (i, j + nj))],
        out_specs=tile,
        out_shape=jax.ShapeDtypeStruct((m, n), BF16),
        compiler_params=_cparams(("parallel", "arbitrary")),
        name="merge",
    )(ret_o, diff_o, w_ret, w_diff, gates, gates)


def _ret_log_decay():
    return jnp.log(1.0 - jnp.exp2(-5.0 - jnp.arange(N_HEADS, dtype=F32)))


def _silu(g):
    return g * jax.nn.sigmoid(g)


def _ret_prompt_body(gl_ref, q_ref, k_ref, v_ref, g_ref, dec_ref, cw_ref, uw_ref, o_ref, st_ref, s_sc):
    c = pl.program_id(0)

    @pl.when(c == 0)
    def _():
        s_sc[...] = jnp.zeros_like(s_sc)

    for h in range(N_HEADS):
        sl = slice(h * HEAD_W, (h + 1) * HEAD_W)
        q, k, v = q_ref[:, sl], k_ref[:, sl], v_ref[:, sl]
        state = s_sc[h]
        scores = _nt(q, k) * dec_ref[h]
        o = jnp.dot(scores.astype(BF16), v, preferred_element_type=F32)
        o = o + jnp.dot(q, state.astype(BF16), preferred_element_type=F32) * cw_ref[h]
        kw = (k.astype(F32) * uw_ref[h]).astype(BF16)
        s_sc[h] = gl_ref[h] * state + lax.dot_general(kw, v, TN_DIMS, preferred_element_type=F32)
        o_ref[:, sl] = (_rms(o) * _silu(g_ref[:, sl].astype(F32))).astype(o_ref.dtype)

    @pl.when(c == pl.num_programs(0) - 1)
    def _():
        st_ref[...] = s_sc[...]


def _ret_prompt(rq, rk, rvg):
    t = rq.shape[0]
    L = RET_L
    log_g = _ret_log_decay()
    idx = jnp.arange(L, dtype=F32)
    diff = idx[:, None] - idx[None, :]
    decay = jnp.where(diff >= 0, jnp.exp(log_g[:, None, None] * jnp.maximum(diff, 0.0)), 0.0)
    cw = jnp.exp(log_g[:, None] * (idx[None, :] + 1.0))
    uw = jnp.exp(log_g[:, None] * (L - 1.0 - idx[None, :]))
    cw = jnp.broadcast_to(cw[:, :, None], (N_HEADS, L, HEAD_W))
    uw = jnp.broadcast_to(uw[:, :, None], (N_HEADS, L, HEAD_W))
    gl = jnp.exp(log_g * L)
    rows = pl.BlockSpec((L, N_HEADS * HEAD_W), lambda c: (c, 0))
    const3 = lambda shape: pl.BlockSpec(shape, lambda c: (0, 0, 0))
    o, st = pl.pallas_call(
        _ret_prompt_body,
        grid=(t // L,),
        in_specs=[pl.BlockSpec(memory_space=pltpu.SMEM), rows, rows, rows,
                  pl.BlockSpec((L, N_HEADS * HEAD_W), lambda c: (c, 1)),
                  const3((N_HEADS, L, L)), const3((N_HEADS, L, HEAD_W)), const3((N_HEADS, L, HEAD_W))],
        out_specs=[rows, const3((N_HEADS, HEAD_W, HEAD_W))],
        out_shape=[jax.ShapeDtypeStruct((t, N_HEADS * HEAD_W), BF16),
                   jax.ShapeDtypeStruct((N_HEADS, HEAD_W, HEAD_W), F32)],
        scratch_shapes=[pltpu.VMEM((N_HEADS, HEAD_W, HEAD_W), F32)],
        compiler_params=_cparams(("arbitrary",)),
        name="ret_prompt",
    )(gl, rq, rk, rvg, rvg, decay, cw, uw)
    return o, st


def _ret_sample_body(gam_ref, q_ref, k_ref, v_ref, g_ref, s_ref, o_ref, sn_ref):
    for h in range(N_HEADS):
        state = s_ref[0, h]
        q, k = q_ref[0, h], k_ref[0, h]
        v = v_ref[0, h]
        gam = gam_ref[h]
        qk = jnp.sum(q * k, axis=0, keepdims=True)
        o = qk * v + gam * jnp.sum(q * state, axis=0, keepdims=True)
        sn_ref[0, h] = gam * state + k * v
        o_ref[0, h] = (_rms(o) * _silu(g_ref[0, h])).astype(o_ref.dtype)


def _ret_sample(rq, rk, rv, rg, state):
    b = rq.shape[0]
    gam = jnp.exp(_ret_log_decay())
    col = lambda a: a.reshape(b, N_HEADS, HEAD_W, 1)
    rowv = lambda a: a.reshape(b, N_HEADS, 1, HEAD_W)
    col_spec = pl.BlockSpec((1, N_HEADS, HEAD_W, 1), lambda i: (i, 0, 0, 0))
    row_spec = pl.BlockSpec((1, N_HEADS, 1, HEAD_W), lambda i: (i, 0, 0, 0))
    st_spec = pl.BlockSpec((1, N_HEADS, HEAD_W, HEAD_W), lambda i: (i, 0, 0, 0))
    o, sn = pl.pallas_call(
        _ret_sample_body,
        grid=(b,),
        in_specs=[pl.BlockSpec(memory_space=pltpu.SMEM), col_spec, col_spec, row_spec, row_spec, st_spec],
        out_specs=[row_spec, st_spec],
        out_shape=[jax.ShapeDtypeStruct((b, N_HEADS, 1, HEAD_W), BF16),
                   jax.ShapeDtypeStruct(state.shape, F32)],
        compiler_params=_cparams(("parallel",)),
        name="ret_sample",
    )(gam, col(rq), col(rk), rowv(rv), rowv(rg), state)
    return o.reshape(b, N_HEADS * HEAD_W), sn


def _lambda(lv_ref, lam_init):
    lv = lv_ref[...]
    e1 = jnp.exp(jnp.sum(lv[0:1] * lv[1:2], axis=-1, keepdims=True))
    e2 = jnp.exp(jnp.sum(lv[2:3] * lv[3:4], axis=-1, keepdims=True))
    return e1 - e2 + lam_init


def _diff_prompt_body(q_ref, k_ref, v_ref, lv_ref, g_ref, o_ref, acc1, acc2, *, tq, lam_init):
    qi = pl.program_id(1)
    q1, q2 = q_ref[:, :DIFF_D], q_ref[:, DIFF_D:]

    def update(s, m, l, acc, v):
        m_new = jnp.maximum(m, jnp.max(s, axis=-1, keepdims=True))
        alpha = jnp.exp(m - m_new)
        p = jnp.exp(s - m_new)
        l_new = alpha * l + jnp.sum(p, axis=-1, keepdims=True)
        acc[...] = alpha * acc[...] + jnp.dot(p.astype(BF16), v, preferred_element_type=F32)
        return m_new, l_new

    def step(kv, carry, masked):
        m1, l1, m2, l2 = carry
        ks = pl.multiple_of(kv * tq, tq)
        k = k_ref[pl.ds(ks, tq), :]
        v = v_ref[pl.ds(ks, tq), :]
        s1 = _nt(q1, k[:, :DIFF_D])
        s2 = _nt(q2, k[:, DIFF_D:])
        if masked:
            keep = (lax.broadcasted_iota(jnp.int32, (tq, tq), 1) <= lax.broadcasted_iota(jnp.int32, (tq, tq), 0))
            s1 = jnp.where(keep, s1, -jnp.inf)
            s2 = jnp.where(keep, s2, -jnp.inf)
        m1, l1 = update(s1, m1, l1, acc1, v)
        m2, l2 = update(s2, m2, l2, acc2, v)
        return m1, l1, m2, l2

    acc1[...] = jnp.zeros_like(acc1)
    acc2[...] = jnp.zeros_like(acc2)
    neg = jnp.full((tq, 1), -jnp.inf, F32)
    zero = jnp.zeros((tq, 1), F32)
    carry = lax.fori_loop(0, qi, lambda kv, c: step(kv, c, False), (neg, zero, neg, zero))
    m1, l1, m2, l2 = step(qi, carry, True)
    lam = _lambda(lv_ref, lam_init)
    o = acc1[...] / l1 - lam * (acc2[...] / l2)
    o_ref[...] = (_rms(o) * g_ref[...] * (1.0 - lam_init)).astype(o_ref.dtype)


def _diff_prompt(dq, dk, dv, lam_vecs, subln, lam_init, tq=512):
    t = dq.shape[0]
    tq = min(tq, t)
    qspec = pl.BlockSpec((tq, HEAD_W), lambda h, i: (i, h))
    kvspec = pl.BlockSpec((t, HEAD_W), lambda h, i: (0, h))
    return pl.pallas_call(
        functools.partial(_diff_prompt_body, tq=tq, lam_init=lam_init),
        grid=(N_HEADS, t // tq),
        in_specs=[qspec, kvspec, kvspec,
                  pl.BlockSpec((4, DIFF_D), lambda h, i: (0, 0)),
                  pl.BlockSpec((1, HEAD_W), lambda h, i: (0, 0))],
        out_specs=qspec,
        out_shape=jax.ShapeDtypeStruct((t, N_HEADS * HEAD_W), BF16),
        scratch_shapes=[pltpu.VMEM((tq, HEAD_W), F32), pltpu.VMEM((tq, HEAD_W), F32)],
        compiler_params=_cparams(("parallel", "arbitrary")),
        name="diff_prompt",
    )(dq, dk, dv, lam_vecs, subln.reshape(1, HEAD_W))


PAGES_PER_STEP = 4


def _diff_sample_body(pt_ref, q_ref, kn_ref, vn_ref, *rest, page, lam_init):
    ck_refs, cv_refs = rest[:PAGES_PER_STEP], rest[PAGES_PER_STEP:2 * PAGES_PER_STEP]
    lv_ref, g_ref, o_ref, m_sc, l_sc, acc_sc = rest[2 * PAGES_PER_STEP:]
    p = pl.program_id(1)

    @pl.when(p == 0)
    def _():
        m_sc[...] = jnp.full_like(m_sc, -jnp.inf)
        l_sc[...] = jnp.zeros_like(l_sc)
        acc_sc[...] = jnp.zeros_like(acc_sc)

    def fold(n, s):
        m_old = m_sc[n][:, :1]
        m_new = jnp.maximum(m_old, jnp.max(s, axis=-1, keepdims=True))
        alpha = jnp.exp(m_old - m_new)
        pr = jnp.exp(s - m_new)
        l_new = alpha * l_sc[n][:, :1] + jnp.sum(pr, axis=-1, keepdims=True)
        m_sc[n] = jnp.broadcast_to(m_new, m_sc.shape[1:])
        l_sc[n] = jnp.broadcast_to(l_new, l_sc.shape[1:])
        return alpha, pr

    rows = page * N_HEADS
    own_head = (lax.broadcasted_iota(jnp.int32, (N_HEADS, rows), 1) % N_HEADS
                == lax.broadcasted_iota(jnp.int32, (N_HEADS, rows), 0))
    q_both = jnp.concatenate([q_ref[0, 0], q_ref[0, 1]], axis=0).astype(BF16)
    alphas, weights = [], []
    for n in range(2):
        s_pages = []
        for ck_ref in ck_refs:
            k_rows = ck_ref[0, pl.ds(n, rows, stride=2), :].astype(BF16)
            s_pages.append(jnp.where(own_head, _nt(q_both, k_rows)[n * N_HEADS:(n + 1) * N_HEADS], -jnp.inf))
        alpha, pr = fold(n, jnp.concatenate(s_pages, axis=1))
        alphas.append(alpha)
        weights.append(pr)
    w_both = jnp.concatenate(weights, axis=0).astype(BF16)
    pv = jnp.zeros((2 * N_HEADS, HEAD_W), F32)
    for j, cv_ref in enumerate(cv_refs):
        pv += jnp.dot(w_both[:, j * rows:(j + 1) * rows], cv_ref[0].astype(BF16), preferred_element_type=F32)
    for n in range(2):
        acc_sc[n] = alphas[n] * acc_sc[n] + pv[n * N_HEADS:(n + 1) * N_HEADS]

    @pl.when(p == pl.num_programs(1) - 1)
    def _():
        outs = []
        for n in range(2):
            s = jnp.sum(kn_ref[0, n] * q_ref[0, n], axis=-1, keepdims=True)
            alpha, pr = fold(n, s)
            acc_sc[n] = alpha * acc_sc[n] + pr * vn_ref[0]
            outs.append(acc_sc[n] / l_sc[n][:, :1])
        o = outs[0] - _lambda(lv_ref, lam_init) * outs[1]
        o_ref[0] = (_rms(o) * g_ref[...] * (1.0 - lam_init)).astype(o_ref.dtype)


def _diff_sample(dq, dk, dv, cache_k, cache_v, page_table, lam_vecs, subln, lam_init):
    b, npg = page_table.shape
    n_pool, page = cache_k.shape[:2]
    pps = PAGES_PER_STEP
    assert npg % pps == 0
    to_maps = lambda a: a.reshape(b, N_HEADS, 2, DIFF_D).transpose(0, 2, 1, 3)
    ck = cache_k.reshape(n_pool, page * N_HEADS * 2, DIFF_D)
    cv = cache_v.reshape(n_pool, page * N_HEADS, HEAD_W)
    qk_spec = pl.BlockSpec((1, 2, N_HEADS, DIFF_D), lambda i, p, pt: (i, 0, 0, 0))
    hv_spec = pl.BlockSpec((1, N_HEADS, HEAD_W), lambda i, p, pt: (i, 0, 0))

    def page_spec(shape, j):
        return pl.BlockSpec((1,) + shape, lambda i, p, pt: (pt[i * npg + p * pps + j], 0, 0))

    grid_spec = pltpu.PrefetchScalarGridSpec(
        num_scalar_prefetch=1,
        grid=(b, npg // pps),
        in_specs=([qk_spec, qk_spec, hv_spec]
                  + [page_spec((page * N_HEADS * 2, DIFF_D), j) for j in range(pps)]
                  + [page_spec((page * N_HEADS, HEAD_W), j) for j in range(pps)]
                  + [pl.BlockSpec((4, DIFF_D), lambda i, p, pt: (0, 0)),
                     pl.BlockSpec((1, HEAD_W), lambda i, p, pt: (0, 0))]),
        out_specs=hv_spec,
        scratch_shapes=[pltpu.VMEM((2, N_HEADS, LANES), F32), pltpu.VMEM((2, N_HEADS, LANES), F32),
                        pltpu.VMEM((2, N_HEADS, HEAD_W), F32)],
    )
    o = pl.pallas_call(
        functools.partial(_diff_sample_body, page=page, lam_init=lam_init),
        grid_spec=grid_spec,
        out_shape=jax.ShapeDtypeStruct((b, N_HEADS, HEAD_W), BF16),
        compiler_params=_cparams(("parallel", "arbitrary")),
        name="diff_sample",
    )(page_table.reshape(-1), to_maps(dq), to_maps(dk), dv.reshape(b, N_HEADS, HEAD_W),
      *([ck] * pps), *([cv] * pps), lam_vecs, subln.reshape(1, HEAD_W))
    return o.reshape(b, N_HEADS * HEAD_W)


W_PITCH = PEER_NKEYS + 8


def _topk_rows(x, k, ids=None):
    if ids is None:
        ids = lax.broadcasted_iota(jnp.int32, x.shape, 0).astype(F32)
    vals, idxs = [], []
    for _ in range(k):
        m = jnp.max(x, axis=0, keepdims=True)
        idx = jnp.min(jnp.where(x == m, ids, 1e9), axis=0, keepdims=True)
        x = jnp.where(ids == idx, -jnp.inf, x)
        vals.append(m)
        idxs.append(idx)
    return jnp.concatenate(vals, axis=0), jnp.concatenate(idxs, axis=0)


def _pair_candidates(s1, s2):
    k, t = s1.shape
    assert k == 16
    blocks = [s1[0:1] + s2] + [s1[i:i + 1] + s2[0:8] for i in range(1, 8)] + [s1[8:16] + s2[0:1]]
    r = lax.broadcasted_iota(jnp.int32, (80, t), 0)
    mid = (((r - 16) >> 3) + 1) * k + ((r - 16) & 7)
    ids = jnp.where(r < 16, r, jnp.where(r < 72, mid, (r - 64) * k))
    return jnp.concatenate(blocks, axis=0), ids.astype(F32)


def _pick_rows(table, sel):
    out = jnp.zeros(sel.shape, F32)
    for r in range(table.shape[0]):
        out = jnp.where(sel == float(r), table[r:r + 1], out)
    return out


def _peer_route_body(q_ref, keys_ref, w_ref, a_sc, b_sc, g_sc, wbuf, *, tm):
    qb = q_ref[...].astype(BF16)
    a_rows, b_rows, g_rows = [], [], []
    for h in range(N_HEADS):
        halves = []
        for n in range(2):
            hn = 2 * h + n
            s_t = _nt(keys_ref[hn], qb[:, hn * LANES:(hn + 1) * LANES])
            halves.append(_topk_rows(s_t, PEER_TOPK))
        (s1, i1), (s2, i2) = halves
        cand, cand_ids = _pair_candidates(s1, s2)
        top, sel = _topk_rows(cand, PEER_TOPK, cand_ids)
        sel_i = jnp.floor(sel * (1.0 / PEER_TOPK))
        sel_j = sel - sel_i * PEER_TOPK
        a_rows.append(_pick_rows(i1, sel_i))
        b_rows.append(_pick_rows(i2, sel_j))
        e = jnp.exp(top - top[0:1])
        g_rows.append(e / jnp.sum(e, axis=0, keepdims=True))
    a_sc[...] = jnp.concatenate(a_rows, axis=0).T
    b_sc[...] = jnp.concatenate(b_rows, axis=0).T
    g_sc[...] = jnp.concatenate(g_rows, axis=0).T

    sub = lax.broadcasted_iota(jnp.int32, (PEER_NKEYS, N_HEADS * PEER_TOPK), 0).astype(F32)

    def token(t, carry):
        ar, br, gr = a_sc[pl.ds(t, 1), :], b_sc[pl.ds(t, 1), :], g_sc[pl.ds(t, 1), :]
        p_mat = jnp.where(sub == ar, 1.0, 0.0).astype(BF16)
        q_mat = jnp.where(sub == br, gr, 0.0).astype(BF16)
        wbuf[pl.ds(pl.multiple_of(t * W_PITCH, 8), PEER_NKEYS), :] = _nt(p_mat, q_mat)
        return carry

    lax.fori_loop(0, tm, token, 0, unroll=8)
    for a in range(PEER_NKEYS):
        w_ref[a] = wbuf[pl.ds(a, tm, stride=W_PITCH), :].astype(w_ref.dtype)


def _peer_route(q, keys, tm=128):
    m = q.shape[0]
    tm = min(tm, m)
    slots = N_HEADS * PEER_TOPK
    return pl.pallas_call(
        functools.partial(_peer_route_body, tm=tm),
        grid=(m // tm,),
        in_specs=[pl.BlockSpec((tm, q.shape[1]), lambda i: (i, 0)),
                  pl.BlockSpec(keys.shape, lambda i: (0, 0, 0))],
        out_specs=pl.BlockSpec((PEER_NKEYS, tm, PEER_NKEYS), lambda i: (0, i, 0)),
        out_shape=jax.ShapeDtypeStruct((PEER_NKEYS, m, PEER_NKEYS), BF16),
        scratch_shapes=[pltpu.VMEM((tm, slots), F32), pltpu.VMEM((tm, slots), F32), pltpu.VMEM((tm, slots), F32),
                        pltpu.VMEM((tm * W_PITCH, PEER_NKEYS), F32)],
        compiler_params=_cparams(("parallel",)),
        name="peer_route",
    )(q, keys)


def _peer_dense_body(h_ref, u_ref, v_ref, w_ref, o_ref, *, ta, td):
    n = pl.program_id(1)

    @pl.when(n == 0)
    def _():
        o_ref[...] = jnp.zeros_like(o_ref)

    s = _nt(h_ref[...], u_ref[...])
    act = 0.5 * s * (1.0 + lax.erf(s * INV_SQRT2))
    c = jnp.concatenate([act[:, a * LANES:(a + 1) * LANES] * w_ref[a].astype(F32) for a in range(ta)], axis=1)
    c = c.astype(BF16)
    for d0 in range(0, o_ref.shape[1], td):
        o_ref[:, d0:d0 + td] += jnp.dot(c, v_ref[:, d0:d0 + td], preferred_element_type=F32)


def _peer_dense(hn, u, v, w, tm=512, tn=1024, td=1024):
    m, d = hn.shape
    n_exp = u.shape[0]
    tm = min(tm, m)
    ta = tn // LANES
    once = pl.Buffered(1)
    return pl.pallas_call(
        functools.partial(_peer_dense_body, ta=ta, td=td),
        grid=(m // tm, n_exp // tn),
        in_specs=[pl.BlockSpec((tm, d), lambda i, n: (i, 0), pipeline_mode=once),
                  pl.BlockSpec((tn, d), lambda i, n: (n, 0)),
                  pl.BlockSpec((tn, d), lambda i, n: (n, 0)),
                  pl.BlockSpec((ta, tm, LANES), lambda i, n: (n, i, 0))],
        out_specs=pl.BlockSpec((tm, d), lambda i, n: (i, 0), pipeline_mode=once),
        out_shape=jax.ShapeDtypeStruct((m, d), F32),
        compiler_params=_cparams(("parallel", "arbitrary")),
        name="peer_dense",
    )(hn, u, v, w)


def _rope_tables(pos, d):
    inv = ROPE_THETA ** (-jnp.arange(0, d, 2, dtype=F32) / d)
    ang = pos.astype(F32)[:, None] * inv[None, :]
    return jnp.cos(ang), jnp.sin(ang)


def _layer(x, pos, is_prompt, wts, lam_init, mixer_state):
    (norm_mix, w_in, lam_vecs, subln, w_ret_out, w_diff_out, w_o, norm_ffn, peer_w_q, peer_keys,
     peer_u, peer_v) = wts
    m, d = x.shape
    seg = N_HEADS * HEAD_W
    act_dt = BF16 if is_prompt else F32
    xn = _rmsnorm(x, norm_mix, BF16)
    c256, s256 = _rope_tables(pos, HEAD_W)
    c128, s128 = _rope_tables(pos, DIFF_D)
    cc, ss = jnp.concatenate([c128, c128], axis=1), jnp.concatenate([-s128, s128], axis=1)
    rope_r = ((c256, "row"), (s256, "row"))
    rope_d = ((cc, "row"), (ss, "row"))

    def seg_proj(i, n, epi, dts, **kw):
        return _proj(xn, w_in, i * seg, n * seg, epi, dts, name=f"in_proj_{i}", **kw)

    (rq,) = seg_proj(0, 1, "rope256", [act_dt], extras=rope_r)
    (rk,) = seg_proj(1, 1, "rope256", [act_dt], extras=rope_r, scale=HEAD_W ** -0.5)
    (rvg,) = seg_proj(2, 2, "plain", [act_dt])
    (dq,) = seg_proj(4, 1, "rope128", [act_dt], extras=rope_d, scale=DIFF_D ** -0.5)
    dk_f, dk_b = seg_proj(5, 1, "rope128", [F32, BF16], extras=rope_d)
    dv_f, dv_b = seg_proj(6, 1, "plain", [F32, BF16])
    (gates,) = seg_proj(7, 4, "sigmoid", [BF16])

    if is_prompt:
        ret_o, ret_state = _ret_prompt(rq, rk, rvg)
        diff_o = _diff_prompt(dq, dk_b, dv_b, lam_vecs, subln, lam_init)
    else:
        state, cache_k, cache_v, page_table = mixer_state
        ret_o, ret_state = _ret_sample(rq, rk, rvg[:, :seg], rvg[:, seg:], state)
        diff_o = _diff_sample(dq, dk_f, dv_f, cache_k, cache_v, page_table, lam_vecs, subln, lam_init)

    merged = _merge(ret_o, diff_o, w_ret_out, w_diff_out, gates)
    (x1,) = _proj(merged, w_o, 0, d, "residual", [F32], extras=((x, "tile"),), name="out_proj")
    hn = _rmsnorm(x1, norm_ffn, BF16)
    (pq,) = _proj(hn, peer_w_q, 0, peer_w_q.shape[1], "plain", [F32], name="peer_q")
    w = _peer_route(pq, peer_keys)
    y = _peer_dense(hn, peer_u, peer_v, w)
    return x1, y, ret_state, dk_f, dv_f


def kernel(x_prompt, x_sample, cache_k, cache_v, state_ret, page_table, norm_mix, w_in, lambda_q1, lambda_k1,
           lambda_q2, lambda_k2, diff_subln, w_ret_out, w_diff_out, w_o, norm_ffn, peer_w_q, peer_keys, peer_u,
           peer_v, norm_final):
    depth = w_in.shape[0]
    assert depth == 1
    b_p, t_p, d = x_prompt.shape
    b_s, t_s, _ = x_sample.shape
    assert b_p == 1 and t_s == 1
    past = page_table.shape[1] * cache_k.shape[2]
    l = 0
    lam_init = 0.8 - 0.6 * math.exp(-0.3 * l)
    lam_vecs = jnp.stack([lambda_q1[l], lambda_k1[l], lambda_q2[l], lambda_k2[l]])
    keys = peer_keys[l].reshape(N_HEADS * 2, PEER_NKEYS, -1).astype(BF16)
    wts = (norm_mix[l], w_in[l], lam_vecs, diff_subln[l], w_ret_out[l], w_diff_out[l], w_o[l], norm_ffn[l],
           peer_w_q[l], keys, peer_u[l].astype(BF16), peer_v[l].astype(BF16))

    xp = x_prompt.reshape(t_p, d)
    xs = x_sample.reshape(b_s, d)
    x1p, yp, sp, kp, vp = _layer(xp, jnp.arange(t_p), True, wts, lam_init, None)
    x1s, ys, ss, ks, vs = _layer(xs, jnp.full((b_s,), past), False, wts, lam_init,
                                 (state_ret[l], cache_k[l], cache_v[l], page_table))
    y_prompt = _add_rmsnorm(x1p, yp, norm_final).reshape(b_p, t_p, d)
    y_sample = _add_rmsnorm(x1s, ys, norm_final).reshape(b_s, t_s, d)
    k_prompt = kp.reshape(1, b_p, t_p, N_HEADS, 2, DIFF_D)
    v_prompt = vp.reshape(1, b_p, t_p, N_HEADS, HEAD_W)
    k_sample = ks.reshape(1, b_s, t_s, N_HEADS, 2, DIFF_D)
    v_sample = vs.reshape(1, b_s, t_s, N_HEADS, HEAD_W)
    return (y_prompt, y_sample, k_prompt, v_prompt, sp[None, None], k_sample, v_sample, ss[None])
```

```python
import functools
import math

import jax
import jax.numpy as jnp
import numpy as np
from jax import lax
from jax.experimental import pallas as pl
from jax.experimental.pallas import tpu as pltpu

F32 = jnp.float32
BF16 = jnp.bfloat16

LANES = 128
VMEM_LIMIT = 56 * 1024 * 1024

EPS = 1e-6
ROPE_THETA = 10000.0
N_HEADS = 8
HEAD_W = 256
DIFF_D = 128
PEER_NKEYS = 128
PEER_TOPK = 16
RET_L = 256
INV_SQRT2 = 0.7071067811865476
LOG2E = 1.4426950408889634

NT_DIMS = (((1,), (1,)), ((), ()))
TN_DIMS = (((0,), (0,)), ((), ()))


def _cparams(sem):
    return pltpu.CompilerParams(dimension_semantics=sem, vmem_limit_bytes=VMEM_LIMIT)


def _nt(a, b):
    return lax.dot_general(a, b, NT_DIMS, preferred_element_type=F32)


def _rms(x):
    return x * lax.rsqrt(jnp.mean(x * x, axis=-1, keepdims=True) + EPS)


def _rmsnorm_body(x_ref, g_ref, o_ref):
    x = x_ref[...]
    o_ref[...] = (_rms(x) * g_ref[...]).astype(o_ref.dtype)


def _rmsnorm(x, g, out_dtype, tm=256):
    m, d = x.shape
    tm = min(tm, m)
    return pl.pallas_call(
        _rmsnorm_body,
        grid=(m // tm,),
        in_specs=[pl.BlockSpec((tm, d), lambda i: (i, 0)), pl.BlockSpec((1, d), lambda i: (0, 0))],
        out_specs=pl.BlockSpec((tm, d), lambda i: (i, 0)),
        out_shape=jax.ShapeDtypeStruct((m, d), out_dtype),
        compiler_params=_cparams(("parallel",)),
        name="rmsnorm",
    )(x, g.reshape(1, d))


def _add_rmsnorm_body(x_ref, y_ref, g_ref, o_ref):
    o_ref[...] = (_rms(x_ref[...] + y_ref[...]) * g_ref[...]).astype(o_ref.dtype)


def _add_rmsnorm(x, y, g, tm=256):
    m, d = x.shape
    tm = min(tm, m)
    row = pl.BlockSpec((tm, d), lambda i: (i, 0))
    return pl.pallas_call(
        _add_rmsnorm_body,
        grid=(m // tm,),
        in_specs=[row, row, pl.BlockSpec((1, d), lambda i: (0, 0))],
        out_specs=row,
        out_shape=jax.ShapeDtypeStruct((m, d), F32),
        compiler_params=_cparams(("parallel",)),
        name="add_rmsnorm",
    )(x, y, g.reshape(1, d))


def _proj_body(x_ref, w_ref, *rest, epi, scale, n_extra, tn):
    extras, outs, w_bf = rest[:n_extra], rest[n_extra:-1], rest[-1]

    @pl.when(pl.program_id(1) == 0)
    def _():
        w_bf[...] = w_ref[...].astype(BF16)

    acc = jnp.dot(x_ref[...], w_bf[...], preferred_element_type=F32)
    if epi == "plain":
        res = acc
    elif epi == "sigmoid":
        res = jax.nn.sigmoid(acc)
    elif epi == "residual":
        res = acc + extras[0][...]
    elif epi == "rope256":
        c, s = extras[0][...], extras[1][...]
        parts = []
        for b in range(tn // HEAD_W):
            x1 = acc[:, b * HEAD_W:b * HEAD_W + LANES]
            x2 = acc[:, b * HEAD_W + LANES:(b + 1) * HEAD_W]
            parts += [x1 * c - x2 * s, x2 * c + x1 * s]
        res = jnp.concatenate(parts, axis=1) * scale
    elif epi == "rope128":
        cc, ss = extras[0][...], extras[1][...]
        parts = []
        for b in range(tn // LANES):
            xb = acc[:, b * LANES:(b + 1) * LANES]
            parts.append(xb * cc + pltpu.roll(xb, LANES // 2, 1) * ss)
        res = jnp.concatenate(parts, axis=1) * scale
    else:
        raise ValueError(epi)
    for o in outs:
        o[...] = res.astype(o.dtype)


def _proj(x, w, col0, ncols, epi, out_dtypes, extras=(), scale=1.0, tm=1024, tn=512, name="proj"):
    m, k = x.shape
    tm = min(tm, m)
    assert m % tm == 0 and ncols % tn == 0 and col0 % tn == 0
    j0 = col0 // tn
    in_specs = [pl.BlockSpec((tm, k), lambda j, i: (i, 0)),
                pl.BlockSpec((k, tn), lambda j, i: (0, j + j0))]
    args = [x, w]
    for arr, kind in extras:
        if kind == "row":
            in_specs.append(pl.BlockSpec((tm, arr.shape[1]), lambda j, i: (i, 0)))
        else:
            in_specs.append(pl.BlockSpec((tm, tn), lambda j, i: (i, j)))
        args.append(arr)
    out_spec = pl.BlockSpec((tm, tn), lambda j, i: (i, j))
    outs = pl.pallas_call(
        functools.partial(_proj_body, epi=epi, scale=scale, n_extra=len(extras), tn=tn),
        grid=(ncols // tn, m // tm),
        in_specs=in_specs,
        out_specs=[out_spec] * len(out_dtypes),
        out_shape=[jax.ShapeDtypeStruct((m, ncols), dt) for dt in out_dtypes],
        scratch_shapes=[pltpu.VMEM((k, tn), BF16)],
        compiler_params=_cparams(("parallel", "arbitrary")),
        name=name,
    )(*args)
    return outs


def _merge_body(r_ref, d_ref, wr_ref, wd_ref, ga_ref, gb_ref, o_ref, wr_bf, wd_bf):
    @pl.when(pl.program_id(1) == 0)
    def _():
        wr_bf[...] = wr_ref[...].astype(BF16)
        wd_bf[...] = wd_ref[...].astype(BF16)

    a = jnp.dot(r_ref[...], wr_bf[...], preferred_element_type=F32)
    b = jnp.dot(d_ref[...], wd_bf[...], preferred_element_type=F32)
    o_ref[...] = (ga_ref[...].astype(F32) * a + gb_ref[...].astype(F32) * b).astype(o_ref.dtype)


def _merge(ret_o, diff_o, w_ret, w_diff, gates, tm=1024, tn=512):
    m, k = ret_o.shape
    n = w_ret.shape[1]
    tm = min(tm, m)
    nj = n // tn
    row = pl.BlockSpec((tm, k), lambda j, i: (i, 0))
    wcol = pl.BlockSpec((k, tn), lambda j, i: (0, j))
    tile = pl.BlockSpec((tm, tn), lambda j, i: (i, j))
    return pl.pallas_call(
        _merge_body,
        grid=(nj, m // tm),
        in_specs=[row, row, wcol, wcol, tile, pl.BlockSpec((tm, tn), lambda j, i: (i, j + nj))],
        out_specs=tile,
        out_shape=jax.ShapeDtypeStruct((m, n), BF16),
        scratch_shapes=[pltpu.VMEM((k, tn), BF16), pltpu.VMEM((k, tn), BF16)],
        compiler_params=_cparams(("parallel", "arbitrary")),
        name="merge",
    )(ret_o, diff_o, w_ret, w_diff, gates, gates)


def _ret_log_decay():
    return jnp.log(1.0 - jnp.exp2(-5.0 - jnp.arange(N_HEADS, dtype=F32)))


def _silu(g):
    return g * jax.nn.sigmoid(g)


def _ret_prompt_body(gl_ref, q_ref, k_ref, v_ref, g_ref, dec_ref, cw_ref, uw_ref, o_ref, st_ref, s_sc):
    c = pl.program_id(0)

    @pl.when(c == 0)
    def _():
        s_sc[...] = jnp.zeros_like(s_sc)

    for h in range(N_HEADS):
        sl = slice(h * HEAD_W, (h + 1) * HEAD_W)
        q, k, v = q_ref[:, sl], k_ref[:, sl], v_ref[:, sl]
        state = s_sc[h]
        scores = _nt(q, k) * dec_ref[h]
        o = jnp.dot(scores.astype(BF16), v, preferred_element_type=F32)
        o = o + jnp.dot(q, state.astype(BF16), preferred_element_type=F32) * cw_ref[h]
        kw = (k.astype(F32) * uw_ref[h]).astype(BF16)
        s_sc[h] = gl_ref[h] * state + lax.dot_general(kw, v, TN_DIMS, preferred_element_type=F32)
        o_ref[:, sl] = (_rms(o) * _silu(g_ref[:, sl].astype(F32))).astype(o_ref.dtype)

    @pl.when(c == pl.num_programs(0) - 1)
    def _():
        st_ref[...] = s_sc[...]


def _ret_prompt(rq, rk, rvg):
    t = rq.shape[0]
    L = RET_L
    log_g = _ret_log_decay()
    idx = jnp.arange(L, dtype=F32)
    diff = idx[:, None] - idx[None, :]
    decay = jnp.where(diff >= 0, jnp.exp(log_g[:, None, None] * jnp.maximum(diff, 0.0)), 0.0)
    cw = jnp.exp(log_g[:, None] * (idx[None, :] + 1.0))
    uw = jnp.exp(log_g[:, None] * (L - 1.0 - idx[None, :]))
    cw = jnp.broadcast_to(cw[:, :, None], (N_HEADS, L, HEAD_W))
    uw = jnp.broadcast_to(uw[:, :, None], (N_HEADS, L, HEAD_W))
    gl = jnp.exp(log_g * L)
    rows = pl.BlockSpec((L, N_HEADS * HEAD_W), lambda c: (c, 0))
    const3 = lambda shape: pl.BlockSpec(shape, lambda c: (0, 0, 0))
    o, st = pl.pallas_call(
        _ret_prompt_body,
        grid=(t // L,),
        in_specs=[pl.BlockSpec(memory_space=pltpu.SMEM), rows, rows, rows,
                  pl.BlockSpec((L, N_HEADS * HEAD_W), lambda c: (c, 1)),
                  const3((N_HEADS, L, L)), const3((N_HEADS, L, HEAD_W)), const3((N_HEADS, L, HEAD_W))],
        out_specs=[rows, const3((N_HEADS, HEAD_W, HEAD_W))],
        out_shape=[jax.ShapeDtypeStruct((t, N_HEADS * HEAD_W), BF16),
                   jax.ShapeDtypeStruct((N_HEADS, HEAD_W, HEAD_W), F32)],
        scratch_shapes=[pltpu.VMEM((N_HEADS, HEAD_W, HEAD_W), F32)],
        compiler_params=_cparams(("arbitrary",)),
        name="ret_prompt",
    )(gl, rq, rk, rvg, rvg, decay, cw, uw)
    return o, st


def _ret_sample_body(gam_ref, q_ref, k_ref, v_ref, g_ref, s_ref, o_ref, sn_ref):
    q, k, v = q_ref[0], k_ref[0], v_ref[0]
    qb, vb = q.astype(BF16), v.astype(BF16)
    head = lax.broadcasted_iota(jnp.int32, q.shape, 0)
    qk = jnp.sum(q * k, axis=-1, keepdims=True)
    outs = []
    for h in range(N_HEADS):
        state = s_ref[0, h]
        gam = gam_ref[h]
        cross = jnp.dot(qb, state.astype(BF16), preferred_element_type=F32)[h:h + 1]
        outs.append(qk[h:h + 1] * v[h:h + 1] + gam * cross)
        k_only = jnp.where(head == h, k, 0.0).astype(BF16)
        sn_ref[0, h] = gam * state + lax.dot_general(k_only, vb, TN_DIMS, preferred_element_type=F32)
    o = jnp.concatenate(outs, axis=0)
    o_ref[0] = (_rms(o) * _silu(g_ref[0])).astype(o_ref.dtype)


def _ret_sample(rq, rk, rv, rg, state):
    b = rq.shape[0]
    gam = jnp.exp(_ret_log_decay())
    rows = lambda a: a.reshape(b, N_HEADS, HEAD_W)
    row_spec = pl.BlockSpec((1, N_HEADS, HEAD_W), lambda i: (i, 0, 0))
    st_spec = pl.BlockSpec((1, N_HEADS, HEAD_W, HEAD_W), lambda i: (i, 0, 0, 0))
    o, sn = pl.pallas_call(
        _ret_sample_body,
        grid=(b,),
        in_specs=[pl.BlockSpec(memory_space=pltpu.SMEM), row_spec, row_spec, row_spec, row_spec, st_spec],
        out_specs=[row_spec, st_spec],
        out_shape=[jax.ShapeDtypeStruct((b, N_HEADS, HEAD_W), BF16),
                   jax.ShapeDtypeStruct(state.shape, F32)],
        compiler_params=_cparams(("parallel",)),
        name="ret_sample",
    )(gam, rows(rq), rows(rk), rows(rv), rows(rg), state)
    return o.reshape(b, N_HEADS * HEAD_W), sn


def _lambda(lv_ref, lam_init):
    lv = lv_ref[...]
    e1 = jnp.exp(jnp.sum(lv[0:1] * lv[1:2], axis=-1, keepdims=True))
    e2 = jnp.exp(jnp.sum(lv[2:3] * lv[3:4], axis=-1, keepdims=True))
    return e1 - e2 + lam_init


def _diff_prompt_body(q_ref, k_ref, v_ref, lv_ref, g_ref, o_ref, acc1, acc2, *, tq, lam_init):
    qi = pl.program_id(1)
    q1, q2 = q_ref[:, :DIFF_D], q_ref[:, DIFF_D:]

    def step(kv, carry, masked):
        ks = pl.multiple_of(kv * tq, tq)
        k = k_ref[pl.ds(ks, tq), :]
        v = v_ref[pl.ds(ks, tq), :]
        m1, l1, m2, l2 = carry
        s1 = _nt(q1, k[:, :DIFF_D])
        s2 = _nt(q2, k[:, DIFF_D:])
        if masked:
            keep = (lax.broadcasted_iota(jnp.int32, (tq, tq), 1) <= lax.broadcasted_iota(jnp.int32, (tq, tq), 0))
            s1 = jnp.where(keep, s1, -jnp.inf)
            s2 = jnp.where(keep, s2, -jnp.inf)
        m1n = jnp.maximum(m1, jnp.max(s1, axis=-1, keepdims=True))
        m2n = jnp.maximum(m2, jnp.max(s2, axis=-1, keepdims=True))
        p1 = jnp.exp2(s1 - m1n)
        p2 = jnp.exp2(s2 - m2n)
        pv1 = jnp.dot(p1.astype(BF16), v, preferred_element_type=F32)
        pv2 = jnp.dot(p2.astype(BF16), v, preferred_element_type=F32)
        a1 = jnp.exp2(m1 - m1n)
        a2 = jnp.exp2(m2 - m2n)
        l1n = a1 * l1 + jnp.sum(p1, axis=-1, keepdims=True)
        l2n = a2 * l2 + jnp.sum(p2, axis=-1, keepdims=True)
        acc1[...] = a1 * acc1[...] + pv1
        acc2[...] = a2 * acc2[...] + pv2
        return m1n, l1n, m2n, l2n

    acc1[...] = jnp.zeros_like(acc1)
    acc2[...] = jnp.zeros_like(acc2)
    neg = jnp.full((tq, 1), -jnp.inf, F32)
    zero = jnp.zeros((tq, 1), F32)
    carry = lax.fori_loop(0, qi, lambda kv, c: step(kv, c, False), (neg, zero, neg, zero))
    m1, l1, m2, l2 = step(qi, carry, True)
    lam = _lambda(lv_ref, lam_init)
    o = acc1[...] / l1 - lam * (acc2[...] / l2)
    o_ref[...] = (_rms(o) * g_ref[...] * (1.0 - lam_init)).astype(o_ref.dtype)


def _diff_prompt(dq, dk, dv, lam_vecs, subln, lam_init, tq=512):
    t = dq.shape[0]
    tq = min(tq, t)
    qspec = pl.BlockSpec((tq, HEAD_W), lambda h, i: (i, h))
    kvspec = pl.BlockSpec((t, HEAD_W), lambda h, i: (0, h))
    return pl.pallas_call(
        functools.partial(_diff_prompt_body, tq=tq, lam_init=lam_init),
        grid=(N_HEADS, t // tq),
        in_specs=[qspec, kvspec, kvspec,
                  pl.BlockSpec((4, DIFF_D), lambda h, i: (0, 0)),
                  pl.BlockSpec((1, HEAD_W), lambda h, i: (0, 0))],
        out_specs=qspec,
        out_shape=jax.ShapeDtypeStruct((t, N_HEADS * HEAD_W), BF16),
        scratch_shapes=[pltpu.VMEM((tq, HEAD_W), F32), pltpu.VMEM((tq, HEAD_W), F32)],
        compiler_params=_cparams(("parallel", "arbitrary")),
        name="diff_prompt",
    )(dq, dk, dv, lam_vecs, subln.reshape(1, HEAD_W))


PAGES_PER_STEP = 8


def _diff_sample_body(pt_ref, q_ref, kn_ref, vn_ref, *rest, page, lam_init):
    ck_refs, cv_refs = rest[:PAGES_PER_STEP], rest[PAGES_PER_STEP:2 * PAGES_PER_STEP]
    lv_ref, g_ref, o_ref, m_sc, l_sc, acc_sc = rest[2 * PAGES_PER_STEP:]
    p = pl.program_id(1)

    @pl.when(p == 0)
    def _():
        m_sc[...] = jnp.full_like(m_sc, -jnp.inf)
        l_sc[...] = jnp.zeros_like(l_sc)
        acc_sc[...] = jnp.zeros_like(acc_sc)

    def fold(n, s):
        m_old = m_sc[n][:, :1]
        m_new = jnp.maximum(m_old, jnp.max(s, axis=-1, keepdims=True))
        alpha = jnp.exp(m_old - m_new)
        pr = jnp.exp(s - m_new)
        l_new = alpha * l_sc[n][:, :1] + jnp.sum(pr, axis=-1, keepdims=True)
        m_sc[n] = jnp.broadcast_to(m_new, m_sc.shape[1:])
        l_sc[n] = jnp.broadcast_to(l_new, l_sc.shape[1:])
        return alpha, pr

    rows = page * N_HEADS
    own_head = (lax.broadcasted_iota(jnp.int32, (N_HEADS, rows), 1) % N_HEADS
                == lax.broadcasted_iota(jnp.int32, (N_HEADS, rows), 0))
    q_both = jnp.concatenate([q_ref[0, 0], q_ref[0, 1]], axis=0).astype(BF16)
    alphas, weights = [], []
    for n in range(2):
        s_pages = []
        for ck_ref in ck_refs:
            k_rows = ck_ref[0, pl.ds(n, rows, stride=2), :].astype(BF16)
            s_pages.append(jnp.where(own_head, _nt(q_both, k_rows)[n * N_HEADS:(n + 1) * N_HEADS], -jnp.inf))
        alpha, pr = fold(n, jnp.concatenate(s_pages, axis=1))
        alphas.append(alpha)
        weights.append(pr)
    w_both = jnp.concatenate(weights, axis=0).astype(BF16)
    pv = jnp.zeros((2 * N_HEADS, HEAD_W), F32)
    for j, cv_ref in enumerate(cv_refs):
        pv += jnp.dot(w_both[:, j * rows:(j + 1) * rows], cv_ref[0].astype(BF16), preferred_element_type=F32)
    for n in range(2):
        acc_sc[n] = alphas[n] * acc_sc[n] + pv[n * N_HEADS:(n + 1) * N_HEADS]

    @pl.when(p == pl.num_programs(1) - 1)
    def _():
        outs = []
        for n in range(2):
            s = jnp.sum(kn_ref[0, n] * q_ref[0, n], axis=-1, keepdims=True)
            alpha, pr = fold(n, s)
            acc_sc[n] = alpha * acc_sc[n] + pr * vn_ref[0]
            outs.append(acc_sc[n] / l_sc[n][:, :1])
        o = outs[0] - _lambda(lv_ref, lam_init) * outs[1]
        o_ref[0] = (_rms(o) * g_ref[...] * (1.0 - lam_init)).astype(o_ref.dtype)


def _diff_sample(dq, dk, dv, cache_k, cache_v, page_table, lam_vecs, subln, lam_init):
    b, npg = page_table.shape
    n_pool, page = cache_k.shape[:2]
    pps = PAGES_PER_STEP
    assert npg % pps == 0
    to_maps = lambda a: a.reshape(b, N_HEADS, 2, DIFF_D).transpose(0, 2, 1, 3)
    ck = cache_k.reshape(n_pool, page * N_HEADS * 2, DIFF_D)
    cv = cache_v.reshape(n_pool, page * N_HEADS, HEAD_W)
    qk_spec = pl.BlockSpec((1, 2, N_HEADS, DIFF_D), lambda i, p, pt: (i, 0, 0, 0))
    hv_spec = pl.BlockSpec((1, N_HEADS, HEAD_W), lambda i, p, pt: (i, 0, 0))

    def page_spec(shape, j):
        return pl.BlockSpec((1,) + shape, lambda i, p, pt: (pt[i * npg + p * pps + j], 0, 0))

    grid_spec = pltpu.PrefetchScalarGridSpec(
        num_scalar_prefetch=1,
        grid=(b, npg // pps),
        in_specs=([qk_spec, qk_spec, hv_spec]
                  + [page_spec((page * N_HEADS * 2, DIFF_D), j) for j in range(pps)]
                  + [page_spec((page * N_HEADS, HEAD_W), j) for j in range(pps)]
                  + [pl.BlockSpec((4, DIFF_D), lambda i, p, pt: (0, 0)),
                     pl.BlockSpec((1, HEAD_W), lambda i, p, pt: (0, 0))]),
        out_specs=hv_spec,
        scratch_shapes=[pltpu.VMEM((2, N_HEADS, LANES), F32), pltpu.VMEM((2, N_HEADS, LANES), F32),
                        pltpu.VMEM((2, N_HEADS, HEAD_W), F32)],
    )
    o = pl.pallas_call(
        functools.partial(_diff_sample_body, page=page, lam_init=lam_init),
        grid_spec=grid_spec,
        out_shape=jax.ShapeDtypeStruct((b, N_HEADS, HEAD_W), BF16),
        compiler_params=_cparams(("parallel", "arbitrary")),
        name="diff_sample",
    )(page_table.reshape(-1), to_maps(dq), to_maps(dk), dv.reshape(b, N_HEADS, HEAD_W),
      *([ck] * pps), *([cv] * pps), lam_vecs, subln.reshape(1, HEAD_W))
    return o.reshape(b, N_HEADS * HEAD_W)


W_PITCH = PEER_NKEYS + 8


def _topk_rows(x, k, ids=None):
    if ids is None:
        ids = lax.broadcasted_iota(jnp.int32, x.shape, 0).astype(F32)
    vals, idxs = [], []
    for _ in range(k):
        m = jnp.max(x, axis=0, keepdims=True)
        idx = jnp.min(jnp.where(x == m, ids, 1e9), axis=0, keepdims=True)
        x = jnp.where(ids == idx, -jnp.inf, x)
        vals.append(m)
        idxs.append(idx)
    return jnp.concatenate(vals, axis=0), jnp.concatenate(idxs, axis=0)


def _pair_candidates(s1, s2):
    k, t = s1.shape
    assert k == 16
    blocks = [s1[0:1] + s2] + [s1[i:i + 1] + s2[0:8] for i in range(1, 8)] + [s1[8:16] + s2[0:1]]
    r = lax.broadcasted_iota(jnp.int32, (80, t), 0)
    mid = (((r - 16) >> 3) + 1) * k + ((r - 16) & 7)
    ids = jnp.where(r < 16, r, jnp.where(r < 72, mid, (r - 64) * k))
    return jnp.concatenate(blocks, axis=0), ids.astype(F32)


def _pick_rows(table, sel):
    out = jnp.zeros(sel.shape, F32)
    for r in range(table.shape[0]):
        out = jnp.where(sel == float(r), table[r:r + 1], out)
    return out


def _peer_route_body(q_ref, keys_ref, w_ref, a_sc, b_sc, g_sc, wbuf, *, tm):
    qb = q_ref[...].astype(BF16)
    a_rows, b_rows, g_rows = [], [], []
    for h in range(N_HEADS):
        halves = []
        for n in range(2):
            hn = 2 * h + n
            s_t = _nt(keys_ref[hn], qb[:, hn * LANES:(hn + 1) * LANES])
            halves.append(_topk_rows(s_t, PEER_TOPK))
        (s1, i1), (s2, i2) = halves
        cand, cand_ids = _pair_candidates(s1, s2)
        top, sel = _topk_rows(cand, PEER_TOPK, cand_ids)
        sel_i = jnp.floor(sel * (1.0 / PEER_TOPK))
        sel_j = sel - sel_i * PEER_TOPK
        a_rows.append(_pick_rows(i1, sel_i))
        b_rows.append(_pick_rows(i2, sel_j))
        e = jnp.exp(top - top[0:1])
        g_rows.append(e / jnp.sum(e, axis=0, keepdims=True))
    a_sc[...] = jnp.concatenate(a_rows, axis=0).T
    b_sc[...] = jnp.concatenate(b_rows, axis=0).T
    g_sc[...] = jnp.concatenate(g_rows, axis=0).T

    sub = lax.broadcasted_iota(jnp.int32, (PEER_NKEYS, N_HEADS * PEER_TOPK), 0).astype(F32)

    def token(t, carry):
        ar, br, gr = a_sc[pl.ds(t, 1), :], b_sc[pl.ds(t, 1), :], g_sc[pl.ds(t, 1), :]
        p_mat = jnp.where(sub == ar, 1.0, 0.0).astype(BF16)
        q_mat = jnp.where(sub == br, gr, 0.0).astype(BF16)
        wbuf[pl.ds(pl.multiple_of(t * W_PITCH, 8), PEER_NKEYS), :] = _nt(p_mat, q_mat)
        return carry

    lax.fori_loop(0, tm, token, 0, unroll=32)
    for a in range(PEER_NKEYS):
        w_ref[a] = wbuf[pl.ds(a, tm, stride=W_PITCH), :].astype(w_ref.dtype)


def _peer_route(q, keys, tm=128):
    m = q.shape[0]
    tm = min(tm, m)
    slots = N_HEADS * PEER_TOPK
    return pl.pallas_call(
        functools.partial(_peer_route_body, tm=tm),
        grid=(m // tm,),
        in_specs=[pl.BlockSpec((tm, q.shape[1]), lambda i: (i, 0)),
                  pl.BlockSpec(keys.shape, lambda i: (0, 0, 0))],
        out_specs=pl.BlockSpec((PEER_NKEYS, tm, PEER_NKEYS), lambda i: (0, i, 0)),
        out_shape=jax.ShapeDtypeStruct((PEER_NKEYS, m, PEER_NKEYS), BF16),
        scratch_shapes=[pltpu.VMEM((tm, slots), F32), pltpu.VMEM((tm, slots), F32), pltpu.VMEM((tm, slots), F32),
                        pltpu.VMEM((tm * W_PITCH, PEER_NKEYS), F32)],
        compiler_params=_cparams(("parallel",)),
        name="peer_route",
    )(q, keys)


def _peer_dense_body(h_ref, u_ref, v_ref, w_ref, o_ref, *, ta, td):
    n = pl.program_id(1)

    @pl.when(n == 0)
    def _():
        o_ref[...] = jnp.zeros_like(o_ref)

    s = _nt(h_ref[...], u_ref[...])
    act = 0.5 * s * (1.0 + lax.erf(s * INV_SQRT2))
    c = jnp.concatenate([act[:, a * LANES:(a + 1) * LANES] * w_ref[a].astype(F32) for a in range(ta)], axis=1)
    c = c.astype(BF16)
    for d0 in range(0, o_ref.shape[1], td):
        o_ref[:, d0:d0 + td] += jnp.dot(c, v_ref[:, d0:d0 + td], preferred_element_type=F32)


def _peer_dense(hn, u, v, w, tm=512, tn=1024, td=1024):
    m, d = hn.shape
    n_exp = u.shape[0]
    tm = min(tm, m)
    ta = tn // LANES
    once = pl.Buffered(1)
    return pl.pallas_call(
        functools.partial(_peer_dense_body, ta=ta, td=td),
        grid=(m // tm, n_exp // tn),
        in_specs=[pl.BlockSpec((tm, d), lambda i, n: (i, 0), pipeline_mode=once),
                  pl.BlockSpec((tn, d), lambda i, n: (n, 0)),
                  pl.BlockSpec((tn, d), lambda i, n: (n, 0)),
                  pl.BlockSpec((ta, tm, LANES), lambda i, n: (n, i, 0))],
        out_specs=pl.BlockSpec((tm, d), lambda i, n: (i, 0), pipeline_mode=once),
        out_shape=jax.ShapeDtypeStruct((m, d), F32),
        compiler_params=_cparams(("parallel", "arbitrary")),
        name="peer_dense",
    )(hn, u, v, w)


def _rope_tables(pos, d):
    inv = ROPE_THETA ** (-jnp.arange(0, d, 2, dtype=F32) / d)
    ang = pos.astype(F32)[:, None] * inv[None, :]
    return jnp.cos(ang), jnp.sin(ang)


def _layer(x, pos, is_prompt, wts, lam_init, mixer_state):
    (norm_mix, w_in, lam_vecs, subln, w_ret_out, w_diff_out, w_o, norm_ffn, peer_w_q, peer_keys,
     peer_u, peer_v) = wts
    m, d = x.shape
    seg = N_HEADS * HEAD_W
    act_dt = BF16 if is_prompt else F32
    xn = _rmsnorm(x, norm_mix, BF16)
    c256, s256 = _rope_tables(pos, HEAD_W)
    c128, s128 = _rope_tables(pos, DIFF_D)
    cc, ss = jnp.concatenate([c128, c128], axis=1), jnp.concatenate([-s128, s128], axis=1)
    rope_r = ((c256, "row"), (s256, "row"))
    rope_d = ((cc, "row"), (ss, "row"))

    def seg_proj(i, n, epi, dts, **kw):
        return _proj(xn, w_in, i * seg, n * seg, epi, dts, name=f"in_proj_{i}", **kw)

    (rq,) = seg_proj(0, 1, "rope256", [act_dt], extras=rope_r)
    (rk,) = seg_proj(1, 1, "rope256", [act_dt], extras=rope_r, scale=HEAD_W ** -0.5)
    (rvg,) = seg_proj(2, 2, "plain", [act_dt])
    q_scale = DIFF_D ** -0.5 * (LOG2E if is_prompt else 1.0)
    (dq,) = seg_proj(4, 1, "rope128", [act_dt], extras=rope_d, scale=q_scale)
    dk_f, dk_b = seg_proj(5, 1, "rope128", [F32, BF16], extras=rope_d)
    dv_f, dv_b = seg_proj(6, 1, "plain", [F32, BF16])
    (gates,) = seg_proj(7, 4, "sigmoid", [BF16])

    if is_prompt:
        ret_o, ret_state = _ret_prompt(rq, rk, rvg)
        diff_o = _diff_prompt(dq, dk_b, dv_b, lam_vecs, subln, lam_init)
    else:
        state, cache_k, cache_v, page_table = mixer_state
        ret_o, ret_state = _ret_sample(rq, rk, rvg[:, :seg], rvg[:, seg:], state)
        diff_o = _diff_sample(dq, dk_f, dv_f, cache_k, cache_v, page_table, lam_vecs, subln, lam_init)

    merged = _merge(ret_o, diff_o, w_ret_out, w_diff_out, gates)
    (x1,) = _proj(merged, w_o, 0, d, "residual", [F32], extras=((x, "tile"),), name="out_proj")
    hn = _rmsnorm(x1, norm_ffn, BF16)
    (pq,) = _proj(hn, peer_w_q, 0, peer_w_q.shape[1], "plain", [F32], name="peer_q")
    w = _peer_route(pq, peer_keys)
    y = _peer_dense(hn, peer_u, peer_v, w)
    return x1, y, ret_state, dk_f, dv_f


def kernel(x_prompt, x_sample, cache_k, cache_v, state_ret, page_table, norm_mix, w_in, lambda_q1, lambda_k1,
           lambda_q2, lambda_k2, diff_subln, w_ret_out, w_diff_out, w_o, norm_ffn, peer_w_q, peer_keys, peer_u,
           peer_v, norm_final):
    depth = w_in.shape[0]
    assert depth == 1
    b_p, t_p, d = x_prompt.shape
    b_s, t_s, _ = x_sample.shape
    assert b_p == 1 and t_s == 1
    past = page_table.shape[1] * cache_k.shape[2]
    l = 0
    lam_init = 0.8 - 0.6 * math.exp(-0.3 * l)
    lam_vecs = jnp.stack([lambda_q1[l], lambda_k1[l], lambda_q2[l], lambda_k2[l]])
    keys = peer_keys[l].reshape(N_HEADS * 2, PEER_NKEYS, -1).astype(BF16)
    wts = (norm_mix[l], w_in[l], lam_vecs, diff_subln[l], w_ret_out[l], w_diff_out[l], w_o[l], norm_ffn[l],
           peer_w_q[l], keys, peer_u[l].astype(BF16), peer_v[l].astype(BF16))

    xp = x_prompt.reshape(t_p, d)
    xs = x_sample.reshape(b_s, d)
    x1p, yp, sp, kp, vp = _layer(xp, jnp.arange(t_p), True, wts, lam_init, None)
    x1s, ys, ss, ks, vs = _layer(xs, jnp.full((b_s,), past), False, wts, lam_init,
                                 (state_ret[l], cache_k[l], cache_v[l], page_table))
    y_prompt = _add_rmsnorm(x1p, yp, norm_final).reshape(b_p, t_p, d)
    y_sample = _add_rmsnorm(x1s, ys, norm_final).reshape(b_s, t_s, d)
    k_prompt = kp.reshape(1, b_p, t_p, N_HEADS, 2, DIFF_D)
    v_prompt = vp.reshape(1, b_p, t_p, N_HEADS, HEAD_W)
    k_sample = ks.reshape(1, b_s, t_s, N_HEADS, 2, DIFF_D)
    v_sample = vs.reshape(1, b_s, t_s, N_HEADS, HEAD_W)
    return (y_prompt, y_sample, k_prompt, v_prompt, sp[None, None], k_sample, v_sample, ss[None])
```

```python
import functools
import math

import jax
import jax.numpy as jnp
import numpy as np
from jax import lax
from jax.experimental import pallas as pl
from jax.experimental.pallas import tpu as pltpu

F32 = jnp.float32
BF16 = jnp.bfloat16

LANES = 128
VMEM_LIMIT = 56 * 1024 * 1024

EPS = 1e-6
ROPE_THETA = 10000.0
N_HEADS = 8
HEAD_W = 256
DIFF_D = 128
PEER_NKEYS = 128
PEER_TOPK = 16
RET_L = 256
INV_SQRT2 = 0.7071067811865476
LOG2E = 1.4426950408889634

NT_DIMS = (((1,), (1,)), ((), ()))
TN_DIMS = (((0,), (0,)), ((), ()))


def _cparams(sem):
    return pltpu.CompilerParams(dimension_semantics=sem, vmem_limit_bytes=VMEM_LIMIT)


def _nt(a, b):
    return lax.dot_general(a, b, NT_DIMS, preferred_element_type=F32)


def _rms(x):
    return x * lax.rsqrt(jnp.mean(x * x, axis=-1, keepdims=True) + EPS)


def _rmsnorm_body(x_ref, g_ref, o_ref):
    x = x_ref[...]
    o_ref[...] = (_rms(x) * g_ref[...]).astype(o_ref.dtype)


def _rmsnorm(x, g, out_dtype, tm=256):
    m, d = x.shape
    tm = min(tm, m)
    return pl.pallas_call(
        _rmsnorm_body,
        grid=(m // tm,),
        in_specs=[pl.BlockSpec((tm, d), lambda i: (i, 0)), pl.BlockSpec((1, d), lambda i: (0, 0))],
        out_specs=pl.BlockSpec((tm, d), lambda i: (i, 0)),
        out_shape=jax.ShapeDtypeStruct((m, d), out_dtype),
        compiler_params=_cparams(("parallel",)),
        name="rmsnorm",
    )(x, g.reshape(1, d))


def _add_rmsnorm_body(x_ref, y_ref, g_ref, o_ref):
    o_ref[...] = (_rms(x_ref[...] + y_ref[...]) * g_ref[...]).astype(o_ref.dtype)


def _add_rmsnorm(x, y, g, tm=256):
    m, d = x.shape
    tm = min(tm, m)
    row = pl.BlockSpec((tm, d), lambda i: (i, 0))
    return pl.pallas_call(
        _add_rmsnorm_body,
        grid=(m // tm,),
        in_specs=[row, row, pl.BlockSpec((1, d), lambda i: (0, 0))],
        out_specs=row,
        out_shape=jax.ShapeDtypeStruct((m, d), F32),
        compiler_params=_cparams(("parallel",)),
        name="add_rmsnorm",
    )(x, y, g.reshape(1, d))


def _proj_body(x_ref, w_ref, *rest, epi, scale, n_extra, n_side, n_out, tn):
    extras, sides = rest[:n_extra], rest[n_extra:n_extra + n_side]
    outs = rest[n_extra + n_side:n_extra + n_side + n_out]
    side_outs, w_bf = rest[n_extra + n_side + n_out:-1], rest[-1]

    @pl.when(pl.program_id(1) == 0)
    def _():
        w_bf[...] = w_ref[...].astype(BF16)

    for src, dst in zip(sides, side_outs):
        dst[...] = src[...].astype(dst.dtype)

    acc = jnp.dot(x_ref[...], w_bf[...], preferred_element_type=F32)
    if epi == "plain":
        res = acc
    elif epi == "sigmoid":
        res = jax.nn.sigmoid(acc)
    elif epi == "residual":
        res = acc + extras[0][...]
    elif epi == "rope256":
        c, s = extras[0][...], extras[1][...]
        parts = []
        for b in range(tn // HEAD_W):
            x1 = acc[:, b * HEAD_W:b * HEAD_W + LANES]
            x2 = acc[:, b * HEAD_W + LANES:(b + 1) * HEAD_W]
            parts += [x1 * c - x2 * s, x2 * c + x1 * s]
        res = jnp.concatenate(parts, axis=1) * scale
    elif epi == "rope128":
        cc, ss = extras[0][...], extras[1][...]
        parts = []
        for b in range(tn // LANES):
            xb = acc[:, b * LANES:(b + 1) * LANES]
            parts.append(xb * cc + pltpu.roll(xb, LANES // 2, 1) * ss)
        res = jnp.concatenate(parts, axis=1) * scale
    else:
        raise ValueError(epi)
    for o in outs:
        o[...] = res.astype(o.dtype)


def _proj(x, w, col0, ncols, epi, out_dtypes, extras=(), scale=1.0, tm=1024, tn=512, name="proj", round_tables=()):
    m, k = x.shape
    tm = min(tm, m)
    assert m % tm == 0 and ncols % tn == 0 and col0 % tn == 0
    j0 = col0 // tn
    n_i = m // tm
    n_steps = (ncols // tn) * n_i
    in_specs = [pl.BlockSpec((tm, k), lambda j, i: (i, 0)),
                pl.BlockSpec((k, tn), lambda j, i: (0, j + j0))]
    args = [x, w]
    for arr, kind in extras:
        if kind == "row":
            in_specs.append(pl.BlockSpec((tm, arr.shape[1]), lambda j, i: (i, 0)))
        else:
            in_specs.append(pl.BlockSpec((tm, tn), lambda j, i: (i, j)))
        args.append(arr)
    out_spec = pl.BlockSpec((tm, tn), lambda j, i: (i, j))
    side_specs = []
    for tab in round_tables:
        assert tab.shape[0] % n_steps == 0
        side_specs.append(pl.BlockSpec((tab.shape[0] // n_steps, tab.shape[1]), lambda j, i: (j * n_i + i, 0)))
        args.append(tab)
    outs = pl.pallas_call(
        functools.partial(_proj_body, epi=epi, scale=scale, n_extra=len(extras), n_side=len(round_tables),
                          n_out=len(out_dtypes), tn=tn),
        grid=(ncols // tn, n_i),
        in_specs=in_specs + side_specs,
        out_specs=[out_spec] * len(out_dtypes) + side_specs,
        out_shape=([jax.ShapeDtypeStruct((m, ncols), dt) for dt in out_dtypes]
                   + [jax.ShapeDtypeStruct(tab.shape, BF16) for tab in round_tables]),
        scratch_shapes=[pltpu.VMEM((k, tn), BF16)],
        compiler_params=_cparams(("parallel", "arbitrary")),
        name=name,
    )(*args)
    return outs


def _merge_body(r_ref, d_ref, wr_ref, wd_ref, ga_ref, gb_ref, o_ref, wr_bf, wd_bf):
    @pl.when(pl.program_id(1) == 0)
    def _():
        wr_bf[...] = wr_ref[...].astype(BF16)
        wd_bf[...] = wd_ref[...].astype(BF16)

    a = jnp.dot(r_ref[...], wr_bf[...], preferred_element_type=F32)
    b = jnp.dot(d_ref[...], wd_bf[...], preferred_element_type=F32)
    o_ref[...] = (ga_ref[...].astype(F32) * a + gb_ref[...].astype(F32) * b).astype(o_ref.dtype)


def _merge(ret_o, diff_o, w_ret, w_diff, gates, tm=1024, tn=512):
    m, k = ret_o.shape
    n = w_ret.shape[1]
    tm = min(tm, m)
    nj = n // tn
    row = pl.BlockSpec((tm, k), lambda j, i: (i, 0))
    wcol = pl.BlockSpec((k, tn), lambda j, i: (0, j))
    tile = pl.BlockSpec((tm, tn), lambda j, i: (i, j))
    return pl.pallas_call(
        _merge_body,
        grid=(nj, m // tm),
        in_specs=[row, row, wcol, wcol, tile, pl.BlockSpec((tm, tn), lambda j, i: (i, j + nj))],
        out_specs=tile,
        out_shape=jax.ShapeDtypeStruct((m, n), BF16),
        scratch_shapes=[pltpu.VMEM((k, tn), BF16), pltpu.VMEM((k, tn), BF16)],
        compiler_params=_cparams(("parallel", "arbitrary")),
        name="merge",
    )(ret_o, diff_o, w_ret, w_diff, gates, gates)


def _ret_log_decay():
    return jnp.log(1.0 - jnp.exp2(-5.0 - jnp.arange(N_HEADS, dtype=F32)))


def _silu(g):
    return g * jax.nn.sigmoid(g)


def _ret_prompt_body(gl_ref, q_ref, k_ref, v_ref, g_ref, dec_ref, cw_ref, uw_ref, o_ref, st_ref, s_sc):
    c = pl.program_id(0)

    @pl.when(c == 0)
    def _():
        s_sc[...] = jnp.zeros_like(s_sc)

    for h in range(N_HEADS):
        sl = slice(h * HEAD_W, (h + 1) * HEAD_W)
        q, k, v = q_ref[:, sl], k_ref[:, sl], v_ref[:, sl]
        state = s_sc[h]
        scores = _nt(q, k) * dec_ref[h]
        o = jnp.dot(scores.astype(BF16), v, preferred_element_type=F32)
        o = o + jnp.dot(q, state.astype(BF16), preferred_element_type=F32) * cw_ref[h]
        kw = (k.astype(F32) * uw_ref[h]).astype(BF16)
        s_sc[h] = gl_ref[h] * state + lax.dot_general(kw, v, TN_DIMS, preferred_element_type=F32)
        o_ref[:, sl] = (_rms(o) * _silu(g_ref[:, sl].astype(F32))).astype(o_ref.dtype)

    @pl.when(c == pl.num_programs(0) - 1)
    def _():
        st_ref[...] = s_sc[...]


def _ret_prompt(rq, rk, rvg):
    t = rq.shape[0]
    L = RET_L
    log_g = _ret_log_decay()
    idx = jnp.arange(L, dtype=F32)
    diff = idx[:, None] - idx[None, :]
    decay = jnp.where(diff >= 0, jnp.exp(log_g[:, None, None] * jnp.maximum(diff, 0.0)), 0.0)
    cw = jnp.exp(log_g[:, None] * (idx[None, :] + 1.0))
    uw = jnp.exp(log_g[:, None] * (L - 1.0 - idx[None, :]))
    cw = jnp.broadcast_to(cw[:, :, None], (N_HEADS, L, HEAD_W))
    uw = jnp.broadcast_to(uw[:, :, None], (N_HEADS, L, HEAD_W))
    gl = jnp.exp(log_g * L)
    rows = pl.BlockSpec((L, N_HEADS * HEAD_W), lambda c: (c, 0))
    const3 = lambda shape: pl.BlockSpec(shape, lambda c: (0, 0, 0))
    o, st = pl.pallas_call(
        _ret_prompt_body,
        grid=(t // L,),
        in_specs=[pl.BlockSpec(memory_space=pltpu.SMEM), rows, rows, rows,
                  pl.BlockSpec((L, N_HEADS * HEAD_W), lambda c: (c, 1)),
                  const3((N_HEADS, L, L)), const3((N_HEADS, L, HEAD_W)), const3((N_HEADS, L, HEAD_W))],
        out_specs=[rows, const3((N_HEADS, HEAD_W, HEAD_W))],
        out_shape=[jax.ShapeDtypeStruct((t, N_HEADS * HEAD_W), BF16),
                   jax.ShapeDtypeStruct((N_HEADS, HEAD_W, HEAD_W), F32)],
        scratch_shapes=[pltpu.VMEM((N_HEADS, HEAD_W, HEAD_W), F32)],
        compiler_params=_cparams(("arbitrary",)),
        name="ret_prompt",
    )(gl, rq, rk, rvg, rvg, decay, cw, uw)
    return o, st


def _ret_sample_body(gam_ref, q_ref, k_ref, v_ref, g_ref, s_ref, o_ref, sn_ref):
    q, k, v = q_ref[0], k_ref[0], v_ref[0]
    qb, vb = q.astype(BF16), v.astype(BF16)
    head = lax.broadcasted_iota(jnp.int32, q.shape, 0)
    qk = jnp.sum(q * k, axis=-1, keepdims=True)
    outs = []
    for h in range(N_HEADS):
        state = s_ref[0, h]
        gam = gam_ref[h]
        cross = jnp.dot(qb, state.astype(BF16), preferred_element_type=F32)[h:h + 1]
        outs.append(qk[h:h + 1] * v[h:h + 1] + gam * cross)
        k_only = jnp.where(head == h, k, 0.0).astype(BF16)
        sn_ref[0, h] = gam * state + lax.dot_general(k_only, vb, TN_DIMS, preferred_element_type=F32)
    o = jnp.concatenate(outs, axis=0)
    o_ref[0] = (_rms(o) * _silu(g_ref[0])).astype(o_ref.dtype)


def _ret_sample(rq, rk, rv, rg, state):
    b = rq.shape[0]
    gam = jnp.exp(_ret_log_decay())
    rows = lambda a: a.reshape(b, N_HEADS, HEAD_W)
    row_spec = pl.BlockSpec((1, N_HEADS, HEAD_W), lambda i: (i, 0, 0))
    st_spec = pl.BlockSpec((1, N_HEADS, HEAD_W, HEAD_W), lambda i: (i, 0, 0, 0))
    o, sn = pl.pallas_call(
        _ret_sample_body,
        grid=(b,),
        in_specs=[pl.BlockSpec(memory_space=pltpu.SMEM), row_spec, row_spec, row_spec, row_spec, st_spec],
        out_specs=[row_spec, st_spec],
        out_shape=[jax.ShapeDtypeStruct((b, N_HEADS, HEAD_W), BF16),
                   jax.ShapeDtypeStruct(state.shape, F32)],
        compiler_params=_cparams(("parallel",)),
        name="ret_sample",
    )(gam, rows(rq), rows(rk), rows(rv), rows(rg), state)
    return o.reshape(b, N_HEADS * HEAD_W), sn


def _lambda(lv_ref, lam_init):
    lv = lv_ref[...]
    e1 = jnp.exp(jnp.sum(lv[0:1] * lv[1:2], axis=-1, keepdims=True))
    e2 = jnp.exp(jnp.sum(lv[2:3] * lv[3:4], axis=-1, keepdims=True))
    return e1 - e2 + lam_init


def _diff_prompt_body(q_ref, k_ref, v_ref, lv_ref, g_ref, o_ref, acc1, acc2, *, tq, lam_init):
    qi = pl.program_id(1)
    q1, q2 = q_ref[:, :DIFF_D], q_ref[:, DIFF_D:]

    def step(kv, carry, masked):
        ks = pl.multiple_of(kv * tq, tq)
        k = k_ref[pl.ds(ks, tq), :]
        v = v_ref[pl.ds(ks, tq), :]
        m1, l1, m2, l2 = carry
        s1 = _nt(q1, k[:, :DIFF_D])
        s2 = _nt(q2, k[:, DIFF_D:])
        if masked:
            keep = (lax.broadcasted_iota(jnp.int32, (tq, tq), 1) <= lax.broadcasted_iota(jnp.int32, (tq, tq), 0))
            s1 = jnp.where(keep, s1, -jnp.inf)
            s2 = jnp.where(keep, s2, -jnp.inf)
        m1n = jnp.maximum(m1, jnp.max(s1, axis=-1, keepdims=True))
        m2n = jnp.maximum(m2, jnp.max(s2, axis=-1, keepdims=True))
        p1 = jnp.exp2(s1 - m1n)
        p2 = jnp.exp2(s2 - m2n)
        pv1 = jnp.dot(p1.astype(BF16), v, preferred_element_type=F32)
        pv2 = jnp.dot(p2.astype(BF16), v, preferred_element_type=F32)
        a1 = jnp.exp2(m1 - m1n)
        a2 = jnp.exp2(m2 - m2n)
        l1n = a1 * l1 + jnp.sum(p1, axis=-1, keepdims=True)
        l2n = a2 * l2 + jnp.sum(p2, axis=-1, keepdims=True)
        acc1[...] = a1 * acc1[...] + pv1
        acc2[...] = a2 * acc2[...] + pv2
        return m1n, l1n, m2n, l2n

    acc1[...] = jnp.zeros_like(acc1)
    acc2[...] = jnp.zeros_like(acc2)
    neg = jnp.full((tq, 1), -jnp.inf, F32)
    zero = jnp.zeros((tq, 1), F32)
    carry = lax.fori_loop(0, qi, lambda kv, c: step(kv, c, False), (neg, zero, neg, zero))
    m1, l1, m2, l2 = step(qi, carry, True)
    lam = _lambda(lv_ref, lam_init)
    o = acc1[...] / l1 - lam * (acc2[...] / l2)
    o_ref[...] = (_rms(o) * g_ref[...] * (1.0 - lam_init)).astype(o_ref.dtype)


def _diff_prompt(dq, dk, dv, lam_vecs, subln, lam_init, tq=512):
    t = dq.shape[0]
    tq = min(tq, t)
    qspec = pl.BlockSpec((tq, HEAD_W), lambda h, i: (i, h))
    kvspec = pl.BlockSpec((t, HEAD_W), lambda h, i: (0, h))
    return pl.pallas_call(
        functools.partial(_diff_prompt_body, tq=tq, lam_init=lam_init),
        grid=(N_HEADS, t // tq),
        in_specs=[qspec, kvspec, kvspec,
                  pl.BlockSpec((4, DIFF_D), lambda h, i: (0, 0)),
                  pl.BlockSpec((1, HEAD_W), lambda h, i: (0, 0))],
        out_specs=qspec,
        out_shape=jax.ShapeDtypeStruct((t, N_HEADS * HEAD_W), BF16),
        scratch_shapes=[pltpu.VMEM((tq, HEAD_W), F32), pltpu.VMEM((tq, HEAD_W), F32)],
        compiler_params=_cparams(("parallel", "arbitrary")),
        name="diff_prompt",
    )(dq, dk, dv, lam_vecs, subln.reshape(1, HEAD_W))


PAGES_PER_STEP = 8


def _diff_sample_body(pt_ref, q_ref, kn_ref, vn_ref, *rest, page, lam_init):
    ck_refs, cv_refs = rest[:PAGES_PER_STEP], rest[PAGES_PER_STEP:2 * PAGES_PER_STEP]
    lv_ref, g_ref, o_ref, m_sc, l_sc, acc_sc = rest[2 * PAGES_PER_STEP:]
    p = pl.program_id(1)

    @pl.when(p == 0)
    def _():
        m_sc[...] = jnp.full_like(m_sc, -jnp.inf)
        l_sc[...] = jnp.zeros_like(l_sc)
        acc_sc[...] = jnp.zeros_like(acc_sc)

    def fold(n, s):
        m_old = m_sc[n][:, :1]
        m_new = jnp.maximum(m_old, jnp.max(s, axis=-1, keepdims=True))
        alpha = jnp.exp(m_old - m_new)
        pr = jnp.exp(s - m_new)
        l_new = alpha * l_sc[n][:, :1] + jnp.sum(pr, axis=-1, keepdims=True)
        m_sc[n] = jnp.broadcast_to(m_new, m_sc.shape[1:])
        l_sc[n] = jnp.broadcast_to(l_new, l_sc.shape[1:])
        return alpha, pr

    rows = page * N_HEADS
    own_head = (lax.broadcasted_iota(jnp.int32, (N_HEADS, rows), 1) % N_HEADS
                == lax.broadcasted_iota(jnp.int32, (N_HEADS, rows), 0))
    q_both = jnp.concatenate([q_ref[0, 0], q_ref[0, 1]], axis=0).astype(BF16)
    alphas, weights = [], []
    for n in range(2):
        s_pages = []
        for ck_ref in ck_refs:
            k_rows = ck_ref[0, pl.ds(n, rows, stride=2), :].astype(BF16)
            s_pages.append(jnp.where(own_head, _nt(q_both, k_rows)[n * N_HEADS:(n + 1) * N_HEADS], -jnp.inf))
        alpha, pr = fold(n, jnp.concatenate(s_pages, axis=1))
        alphas.append(alpha)
        weights.append(pr)
    w_both = jnp.concatenate(weights, axis=0).astype(BF16)
    pv = jnp.zeros((2 * N_HEADS, HEAD_W), F32)
    for j, cv_ref in enumerate(cv_refs):
        pv += jnp.dot(w_both[:, j * rows:(j + 1) * rows], cv_ref[0].astype(BF16), preferred_element_type=F32)
    for n in range(2):
        acc_sc[n] = alphas[n] * acc_sc[n] + pv[n * N_HEADS:(n + 1) * N_HEADS]

    @pl.when(p == pl.num_programs(1) - 1)
    def _():
        outs = []
        for n in range(2):
            s = jnp.sum(kn_ref[0, n] * q_ref[0, n], axis=-1, keepdims=True)
            alpha, pr = fold(n, s)
            acc_sc[n] = alpha * acc_sc[n] + pr * vn_ref[0]
            outs.append(acc_sc[n] / l_sc[n][:, :1])
        o = outs[0] - _lambda(lv_ref, lam_init) * outs[1]
        o_ref[0] = (_rms(o) * g_ref[...] * (1.0 - lam_init)).astype(o_ref.dtype)


def _diff_sample(dq, dk, dv, cache_k, cache_v, page_table, lam_vecs, subln, lam_init):
    b, npg = page_table.shape
    n_pool, page = cache_k.shape[:2]
    pps = PAGES_PER_STEP
    assert npg % pps == 0
    to_maps = lambda a: a.reshape(b, N_HEADS, 2, DIFF_D).transpose(0, 2, 1, 3)
    ck = cache_k.reshape(n_pool, page * N_HEADS * 2, DIFF_D)
    cv = cache_v.reshape(n_pool, page * N_HEADS, HEAD_W)
    qk_spec = pl.BlockSpec((1, 2, N_HEADS, DIFF_D), lambda i, p, pt: (i, 0, 0, 0))
    hv_spec = pl.BlockSpec((1, N_HEADS, HEAD_W), lambda i, p, pt: (i, 0, 0))

    def page_spec(shape, j):
        return pl.BlockSpec((1,) + shape, lambda i, p, pt: (pt[i * npg + p * pps + j], 0, 0))

    grid_spec = pltpu.PrefetchScalarGridSpec(
        num_scalar_prefetch=1,
        grid=(b, npg // pps),
        in_specs=([qk_spec, qk_spec, hv_spec]
                  + [page_spec((page * N_HEADS * 2, DIFF_D), j) for j in range(pps)]
                  + [page_spec((page * N_HEADS, HEAD_W), j) for j in range(pps)]
                  + [pl.BlockSpec((4, DIFF_D), lambda i, p, pt: (0, 0)),
                     pl.BlockSpec((1, HEAD_W), lambda i, p, pt: (0, 0))]),
        out_specs=hv_spec,
        scratch_shapes=[pltpu.VMEM((2, N_HEADS, LANES), F32), pltpu.VMEM((2, N_HEADS, LANES), F32),
                        pltpu.VMEM((2, N_HEADS, HEAD_W), F32)],
    )
    o = pl.pallas_call(
        functools.partial(_diff_sample_body, page=page, lam_init=lam_init),
        grid_spec=grid_spec,
        out_shape=jax.ShapeDtypeStruct((b, N_HEADS, HEAD_W), BF16),
        compiler_params=_cparams(("parallel", "arbitrary")),
        name="diff_sample",
    )(page_table.reshape(-1), to_maps(dq), to_maps(dk), dv.reshape(b, N_HEADS, HEAD_W),
      *([ck] * pps), *([cv] * pps), lam_vecs, subln.reshape(1, HEAD_W))
    return o.reshape(b, N_HEADS * HEAD_W)


W_PITCH = PEER_NKEYS + 8


def _topk_rows(x, k, ids=None):
    if ids is None:
        ids = lax.broadcasted_iota(jnp.int32, x.shape, 0).astype(F32)
    n = x.shape[0]
    assert n % 8 == 0
    id_slabs = [ids[r:r + 8] for r in range(0, n, 8)]
    vals, idxs = [], []
    for _ in range(k):
        v_l, i_l = [x[r:r + 8] for r in range(0, n, 8)], id_slabs
        while len(v_l) > 1:
            v_n, i_n = [], []
            for a in range(0, len(v_l) - 1, 2):
                first = v_l[a] >= v_l[a + 1]
                v_n.append(jnp.where(first, v_l[a], v_l[a + 1]))
                i_n.append(jnp.where(first, i_l[a], i_l[a + 1]))
            if len(v_l) % 2:
                v_n.append(v_l[-1])
                i_n.append(i_l[-1])
            v_l, i_l = v_n, i_n
        m = jnp.max(v_l[0], axis=0, keepdims=True)
        idx = jnp.min(jnp.where(v_l[0] == m, i_l[0], 1e9), axis=0, keepdims=True)
        x = jnp.where(ids == idx, -jnp.inf, x)
        vals.append(m)
        idxs.append(idx)
    return jnp.concatenate(vals, axis=0), jnp.concatenate(idxs, axis=0)


def _pair_candidates(s1, s2):
    k, t = s1.shape
    assert k == 16
    blocks = [s1[0:1] + s2] + [s1[i:i + 1] + s2[0:8] for i in range(1, 8)] + [s1[8:16] + s2[0:1]]
    r = lax.broadcasted_iota(jnp.int32, (80, t), 0)
    mid = (((r - 16) >> 3) + 1) * k + ((r - 16) & 7)
    ids = jnp.where(r < 16, r, jnp.where(r < 72, mid, (r - 64) * k))
    return jnp.concatenate(blocks, axis=0), ids.astype(F32)


def _pick_rows(table, sel):
    out = jnp.zeros(sel.shape, F32)
    for r in range(table.shape[0]):
        out = jnp.where(sel == float(r), table[r:r + 1], out)
    return out


def _peer_route_body(q_ref, keys_ref, w_ref, a_sc, b_sc, g_sc, wbuf, *, tm):
    qb = q_ref[...].astype(BF16)
    a_rows, b_rows, g_rows = [], [], []
    for h in range(N_HEADS):
        halves = []
        for n in range(2):
            hn = 2 * h + n
            s_t = _nt(keys_ref[hn], qb[:, hn * LANES:(hn + 1) * LANES])
            halves.append(_topk_rows(s_t, PEER_TOPK))
        (s1, i1), (s2, i2) = halves
        cand, cand_ids = _pair_candidates(s1, s2)
        top, sel = _topk_rows(cand, PEER_TOPK, cand_ids)
        sel_i = jnp.floor(sel * (1.0 / PEER_TOPK))
        sel_j = sel - sel_i * PEER_TOPK
        a_rows.append(_pick_rows(i1, sel_i))
        b_rows.append(_pick_rows(i2, sel_j))
        e = jnp.exp(top - top[0:1])
        g_rows.append(e / jnp.sum(e, axis=0, keepdims=True))
    a_sc[...] = jnp.concatenate(a_rows, axis=0).T
    b_sc[...] = jnp.concatenate(b_rows, axis=0).T
    g_sc[...] = jnp.concatenate(g_rows, axis=0).T

    sub = lax.broadcasted_iota(jnp.int32, (PEER_NKEYS, N_HEADS * PEER_TOPK), 0).astype(F32)

    def token(t, carry):
        ar, br, gr = a_sc[pl.ds(t, 1), :], b_sc[pl.ds(t, 1), :], g_sc[pl.ds(t, 1), :]
        p_mat = jnp.where(sub == ar, 1.0, 0.0).astype(BF16)
        q_mat = jnp.where(sub == br, gr, 0.0).astype(BF16)
        wbuf[pl.ds(pl.multiple_of(t * W_PITCH, 8), PEER_NKEYS), :] = _nt(p_mat, q_mat)
        return carry

    lax.fori_loop(0, tm, token, 0, unroll=32)
    for a in range(PEER_NKEYS):
        w_ref[a] = wbuf[pl.ds(a, tm, stride=W_PITCH), :].astype(w_ref.dtype)


def _peer_route(q, keys, tm=128):
    m = q.shape[0]
    tm = min(tm, m)
    slots = N_HEADS * PEER_TOPK
    return pl.pallas_call(
        functools.partial(_peer_route_body, tm=tm),
        grid=(m // tm,),
        in_specs=[pl.BlockSpec((tm, q.shape[1]), lambda i: (i, 0)),
                  pl.BlockSpec(keys.shape, lambda i: (0, 0, 0))],
        out_specs=pl.BlockSpec((PEER_NKEYS, tm, PEER_NKEYS), lambda i: (0, i, 0)),
        out_shape=jax.ShapeDtypeStruct((PEER_NKEYS, m, PEER_NKEYS), BF16),
        scratch_shapes=[pltpu.VMEM((tm, slots), F32), pltpu.VMEM((tm, slots), F32), pltpu.VMEM((tm, slots), F32),
                        pltpu.VMEM((tm * W_PITCH, PEER_NKEYS), F32)],
        compiler_params=_cparams(("parallel",)),
        name="peer_route",
    )(q, keys)


def _peer_dense_body(h_ref, u_ref, v_ref, w_ref, o_ref, *, ta, td):
    n = pl.program_id(1)

    @pl.when(n == 0)
    def _():
        o_ref[...] = jnp.zeros_like(o_ref)

    s = _nt(h_ref[...], u_ref[...])
    act = 0.5 * s * (1.0 + lax.erf(s * INV_SQRT2))
    c = jnp.concatenate([act[:, a * LANES:(a + 1) * LANES] * w_ref[a].astype(F32) for a in range(ta)], axis=1)
    c = c.astype(BF16)
    for d0 in range(0, o_ref.shape[1], td):
        o_ref[:, d0:d0 + td] += jnp.dot(c, v_ref[:, d0:d0 + td], preferred_element_type=F32)


def _peer_dense(hn, u, v, w, tm=512, tn=1024, td=1024):
    m, d = hn.shape
    n_exp = v.shape[0]
    tm = min(tm, m)
    ta = tn // LANES
    once = pl.Buffered(1)
    return pl.pallas_call(
        functools.partial(_peer_dense_body, ta=ta, td=td),
        grid=(m // tm, n_exp // tn),
        in_specs=[pl.BlockSpec((tm, d), lambda i, n: (i, 0), pipeline_mode=once),
                  pl.BlockSpec((tn, d), lambda i, n: (n, 0)),
                  pl.BlockSpec((tn, d), lambda i, n: (n, 0)),
                  pl.BlockSpec((ta, tm, LANES), lambda i, n: (n, i, 0))],
        out_specs=pl.BlockSpec((tm, d), lambda i, n: (i, 0), pipeline_mode=once),
        out_shape=jax.ShapeDtypeStruct((m, d), F32),
        compiler_params=_cparams(("parallel", "arbitrary")),
        name="peer_dense",
    )(hn, u, v, w)


def _rope_tables(pos, d, tiled=False):
    inv = ROPE_THETA ** (-jnp.arange(0, d, 2, dtype=F32) / d)
    if tiled:
        inv = jnp.concatenate([inv, inv])
    ang = pos.astype(F32)[:, None] * inv[None, :]
    if tiled:
        sign = jnp.concatenate([-jnp.ones((d // 2,), F32), jnp.ones((d // 2,), F32)])
        return jnp.cos(ang), jnp.sin(ang) * sign[None, :]
    return jnp.cos(ang), jnp.sin(ang)


def _layer(x, pos, is_prompt, wts, lam_init, mixer_state):
    (norm_mix, w_in, lam_vecs, subln, w_ret_out, w_diff_out, w_o, norm_ffn, peer_w_q, peer_keys,
     peer_u, peer_v) = wts
    m, d = x.shape
    seg = N_HEADS * HEAD_W
    act_dt = BF16 if is_prompt else F32
    xn = _rmsnorm(x, norm_mix, BF16)
    c256, s256 = _rope_tables(pos, HEAD_W)
    cc, ss = _rope_tables(pos, DIFF_D, tiled=True)
    rope_r = ((c256, "row"), (s256, "row"))
    rope_d = ((cc, "row"), (ss, "row"))

    def seg_proj(i, n, epi, dts, **kw):
        return _proj(xn, w_in, i * seg, n * seg, epi, dts, name=f"in_proj_{i}", **kw)

    (rq,) = seg_proj(0, 1, "rope256", [act_dt], extras=rope_r)
    (rk,) = seg_proj(1, 1, "rope256", [act_dt], extras=rope_r, scale=HEAD_W ** -0.5)
    (rvg,) = seg_proj(2, 2, "plain", [act_dt])
    q_scale = DIFF_D ** -0.5 * (LOG2E if is_prompt else 1.0)
    (dq,) = seg_proj(4, 1, "rope128", [act_dt], extras=rope_d, scale=q_scale)
    dk_f, dk_b = seg_proj(5, 1, "rope128", [F32, BF16], extras=rope_d)
    dv_f, dv_b = seg_proj(6, 1, "plain", [F32, BF16])
    if peer_u.dtype == BF16:
        (gates,) = seg_proj(7, 4, "sigmoid", [BF16])
    else:
        gates, peer_u, peer_v = seg_proj(7, 4, "sigmoid", [BF16], round_tables=(peer_u, peer_v))

    if is_prompt:
        ret_o, ret_state = _ret_prompt(rq, rk, rvg)
        diff_o = _diff_prompt(dq, dk_b, dv_b, lam_vecs, subln, lam_init)
    else:
        state, cache_k, cache_v, page_table = mixer_state
        ret_o, ret_state = _ret_sample(rq, rk, rvg[:, :seg], rvg[:, seg:], state)
        diff_o = _diff_sample(dq, dk_f, dv_f, cache_k, cache_v, page_table, lam_vecs, subln, lam_init)

    merged = _merge(ret_o, diff_o, w_ret_out, w_diff_out, gates)
    (x1,) = _proj(merged, w_o, 0, d, "residual", [F32], extras=((x, "tile"),), name="out_proj")
    hn = _rmsnorm(x1, norm_ffn, BF16)
    (pq,) = _proj(hn, peer_w_q, 0, peer_w_q.shape[1], "plain", [F32], name="peer_q")
    w = _peer_route(pq, peer_keys)
    y = _peer_dense(hn, peer_u, peer_v, w)
    return x1, y, ret_state, dk_f, dv_f, (peer_u, peer_v)


def kernel(x_prompt, x_sample, cache_k, cache_v, state_ret, page_table, norm_mix, w_in, lambda_q1, lambda_k1,
           lambda_q2, lambda_k2, diff_subln, w_ret_out, w_diff_out, w_o, norm_ffn, peer_w_q, peer_keys, peer_u,
           peer_v, norm_final):
    depth = w_in.shape[0]
    assert depth == 1
    b_p, t_p, d = x_prompt.shape
    b_s, t_s, _ = x_sample.shape
    assert b_p == 1 and t_s == 1
    past = page_table.shape[1] * cache_k.shape[2]
    l = 0
    lam_init = 0.8 - 0.6 * math.exp(-0.3 * l)
    lam_vecs = jnp.stack([lambda_q1[l], lambda_k1[l], lambda_q2[l], lambda_k2[l]])
    keys = peer_keys[l].reshape(N_HEADS * 2, PEER_NKEYS, -1).astype(BF16)
    wts = (norm_mix[l], w_in[l], lam_vecs, diff_subln[l], w_ret_out[l], w_diff_out[l], w_o[l], norm_ffn[l],
           peer_w_q[l], keys)

    xp = x_prompt.reshape(t_p, d)
    xs = x_sample.reshape(b_s, d)
    x1p, yp, sp, kp, vp, tables = _layer(xp, jnp.arange(t_p), True, wts + (peer_u[l], peer_v[l]), lam_init, None)
    x1s, ys, ss, ks, vs, _ = _layer(xs, jnp.full((b_s,), past), False, wts + tables, lam_init,
                                    (state_ret[l], cache_k[l], cache_v[l], page_table))
    y_prompt = _add_rmsnorm(x1p, yp, norm_final).reshape(b_p, t_p, d)
    y_sample = _add_rmsnorm(x1s, ys, norm_final).reshape(b_s, t_s, d)
    k_prompt = kp.reshape(1, b_p, t_p, N_HEADS, 2, DIFF_D)
    v_prompt = vp.reshape(1, b_p, t_p, N_HEADS, HEAD_W)
    k_sample = ks.reshape(1, b_s, t_s, N_HEADS, 2, DIFF_D)
    v_sample = vs.reshape(1, b_s, t_s, N_HEADS, HEAD_W)
    return (y_prompt, y_sample, k_prompt, v_prompt, sp[None, None], k_sample, v_sample, ss[None])
```

```python
import functools
import math

import jax
import jax.numpy as jnp
import numpy as np
from jax import lax
from jax.experimental import pallas as pl
from jax.experimental.pallas import tpu as pltpu

F32 = jnp.float32
BF16 = jnp.bfloat16

LANES = 128
VMEM_LIMIT = 56 * 1024 * 1024

EPS = 1e-6
ROPE_THETA = 10000.0
N_HEADS = 8
HEAD_W = 256
DIFF_D = 128
PEER_NKEYS = 128
PEER_TOPK = 16
RET_L = 256
INV_SQRT2 = 0.7071067811865476
LOG2E = 1.4426950408889634

NT_DIMS = (((1,), (1,)), ((), ()))
TN_DIMS = (((0,), (0,)), ((), ()))


def _cparams(sem):
    return pltpu.CompilerParams(dimension_semantics=sem, vmem_limit_bytes=VMEM_LIMIT)


def _nt(a, b):
    return lax.dot_general(a, b, NT_DIMS, preferred_element_type=F32)


def _rms(x):
    return x * lax.rsqrt(jnp.mean(x * x, axis=-1, keepdims=True) + EPS)


def _rmsnorm_body(x_ref, g_ref, o_ref):
    x = x_ref[...]
    o_ref[...] = (_rms(x) * g_ref[...]).astype(o_ref.dtype)


def _rmsnorm(x, g, out_dtype, tm=256):
    m, d = x.shape
    tm = min(tm, m)
    return pl.pallas_call(
        _rmsnorm_body,
        grid=(m // tm,),
        in_specs=[pl.BlockSpec((tm, d), lambda i: (i, 0)), pl.BlockSpec((1, d), lambda i: (0, 0))],
        out_specs=pl.BlockSpec((tm, d), lambda i: (i, 0)),
        out_shape=jax.ShapeDtypeStruct((m, d), out_dtype),
        compiler_params=_cparams(("parallel",)),
        name="rmsnorm",
    )(x, g.reshape(1, d))


def _add_rmsnorm_body(x_ref, y_ref, g_ref, o_ref):
    o_ref[...] = (_rms(x_ref[...] + y_ref[...]) * g_ref[...]).astype(o_ref.dtype)


def _add_rmsnorm(x, y, g, tm=256):
    m, d = x.shape
    tm = min(tm, m)
    row = pl.BlockSpec((tm, d), lambda i: (i, 0))
    return pl.pallas_call(
        _add_rmsnorm_body,
        grid=(m // tm,),
        in_specs=[row, row, pl.BlockSpec((1, d), lambda i: (0, 0))],
        out_specs=row,
        out_shape=jax.ShapeDtypeStruct((m, d), F32),
        compiler_params=_cparams(("parallel",)),
        name="add_rmsnorm",
    )(x, y, g.reshape(1, d))


def _proj_body(x_ref, w_ref, *rest, epi, scale, n_extra, n_side, n_out, tn):
    extras, sides = rest[:n_extra], rest[n_extra:n_extra + n_side]
    outs = rest[n_extra + n_side:n_extra + n_side + n_out]
    side_outs, w_bf = rest[n_extra + n_side + n_out:-1], rest[-1]

    @pl.when(pl.program_id(1) == 0)
    def _():
        w_bf[...] = w_ref[...].astype(BF16)

    for src, dst in zip(sides, side_outs):
        dst[...] = src[...].astype(dst.dtype)

    acc = jnp.dot(x_ref[...], w_bf[...], preferred_element_type=F32)
    if epi == "plain":
        res = acc
    elif epi == "sigmoid":
        res = jax.nn.sigmoid(acc)
    elif epi == "residual":
        res = acc + extras[0][...]
    elif epi == "rope256":
        c, s = extras[0][...], extras[1][...]
        parts = []
        for b in range(tn // HEAD_W):
            x1 = acc[:, b * HEAD_W:b * HEAD_W + LANES]
            x2 = acc[:, b * HEAD_W + LANES:(b + 1) * HEAD_W]
            parts += [x1 * c - x2 * s, x2 * c + x1 * s]
        res = jnp.concatenate(parts, axis=1) * scale
    elif epi == "rope128":
        cc, ss = extras[0][...], extras[1][...]
        parts = []
        for b in range(tn // LANES):
            xb = acc[:, b * LANES:(b + 1) * LANES]
            parts.append(xb * cc + pltpu.roll(xb, LANES // 2, 1) * ss)
        res = jnp.concatenate(parts, axis=1) * scale
    else:
        raise ValueError(epi)
    for o in outs:
        o[...] = res.astype(o.dtype)


def _proj(x, w, col0, ncols, epi, out_dtypes, extras=(), scale=1.0, tm=1024, tn=512, name="proj", round_tables=()):
    m, k = x.shape
    tm = min(tm, m)
    assert m % tm == 0 and ncols % tn == 0 and col0 % tn == 0
    j0 = col0 // tn
    n_i = m // tm
    n_steps = (ncols // tn) * n_i
    in_specs = [pl.BlockSpec((tm, k), lambda j, i: (i, 0)),
                pl.BlockSpec((k, tn), lambda j, i: (0, j + j0))]
    args = [x, w]
    for arr, kind in extras:
        if kind == "row":
            in_specs.append(pl.BlockSpec((tm, arr.shape[1]), lambda j, i: (i, 0)))
        else:
            in_specs.append(pl.BlockSpec((tm, tn), lambda j, i: (i, j)))
        args.append(arr)
    out_spec = pl.BlockSpec((tm, tn), lambda j, i: (i, j))
    side_specs = []
    for tab in round_tables:
        assert tab.shape[0] % n_steps == 0
        side_specs.append(pl.BlockSpec((tab.shape[0] // n_steps, tab.shape[1]), lambda j, i: (j * n_i + i, 0)))
        args.append(tab)
    outs = pl.pallas_call(
        functools.partial(_proj_body, epi=epi, scale=scale, n_extra=len(extras), n_side=len(round_tables),
                          n_out=len(out_dtypes), tn=tn),
        grid=(ncols // tn, n_i),
        in_specs=in_specs + side_specs,
        out_specs=[out_spec] * len(out_dtypes) + side_specs,
        out_shape=([jax.ShapeDtypeStruct((m, ncols), dt) for dt in out_dtypes]
                   + [jax.ShapeDtypeStruct(tab.shape, BF16) for tab in round_tables]),
        scratch_shapes=[pltpu.VMEM((k, tn), BF16)],
        compiler_params=_cparams(("parallel", "arbitrary")),
        name=name,
    )(*args)
    return outs


def _merge_body(r_ref, d_ref, wr_ref, wd_ref, ga_ref, gb_ref, o_ref, wr_bf, wd_bf):
    @pl.when(pl.program_id(1) == 0)
    def _():
        wr_bf[...] = wr_ref[...].astype(BF16)
        wd_bf[...] = wd_ref[...].astype(BF16)

    a = jnp.dot(r_ref[...], wr_bf[...], preferred_element_type=F32)
    b = jnp.dot(d_ref[...], wd_bf[...], preferred_element_type=F32)
    o_ref[...] = (ga_ref[...].astype(F32) * a + gb_ref[...].astype(F32) * b).astype(o_ref.dtype)


def _merge(ret_o, diff_o, w_ret, w_diff, gates, tm=1024, tn=512):
    m, k = ret_o.shape
    n = w_ret.shape[1]
    tm = min(tm, m)
    nj = n // tn
    row = pl.BlockSpec((tm, k), lambda j, i: (i, 0))
    wcol = pl.BlockSpec((k, tn), lambda j, i: (0, j))
    tile = pl.BlockSpec((tm, tn), lambda j, i: (i, j))
    return pl.pallas_call(
        _merge_body,
        grid=(nj, m // tm),
        in_specs=[row, row, wcol, wcol, tile, pl.BlockSpec((tm, tn), lambda j, i: (i, j + nj))],
        out_specs=tile,
        out_shape=jax.ShapeDtypeStruct((m, n), BF16),
        scratch_shapes=[pltpu.VMEM((k, tn), BF16), pltpu.VMEM((k, tn), BF16)],
        compiler_params=_cparams(("parallel", "arbitrary")),
        name="merge",
    )(ret_o, diff_o, w_ret, w_diff, gates, gates)


def _ret_log_decay():
    return jnp.log(1.0 - jnp.exp2(-5.0 - jnp.arange(N_HEADS, dtype=F32)))


def _silu(g):
    return g * jax.nn.sigmoid(g)


def _ret_prompt_body(gl_ref, q_ref, k_ref, v_ref, g_ref, dec_ref, cw_ref, uw_ref, o_ref, st_ref, s_sc):
    c = pl.program_id(0)

    @pl.when(c == 0)
    def _():
        s_sc[...] = jnp.zeros_like(s_sc)

    for h in range(N_HEADS):
        sl = slice(h * HEAD_W, (h + 1) * HEAD_W)
        q, k, v = q_ref[:, sl], k_ref[:, sl], v_ref[:, sl]
        state = s_sc[h]
        scores = _nt(q, k) * dec_ref[h]
        o = jnp.dot(scores.astype(BF16), v, preferred_element_type=F32)
        o = o + jnp.dot(q, state.astype(BF16), preferred_element_type=F32) * cw_ref[h]
        kw = (k.astype(F32) * uw_ref[h]).astype(BF16)
        s_sc[h] = gl_ref[h] * state + lax.dot_general(kw, v, TN_DIMS, preferred_element_type=F32)
        o_ref[:, sl] = (_rms(o) * _silu(g_ref[:, sl].astype(F32))).astype(o_ref.dtype)

    @pl.when(c == pl.num_programs(0) - 1)
    def _():
        st_ref[...] = s_sc[...]


def _ret_prompt(rq, rk, rvg):
    t = rq.shape[0]
    L = RET_L
    log_g = _ret_log_decay()
    idx = jnp.arange(L, dtype=F32)
    diff = idx[:, None] - idx[None, :]
    decay = jnp.where(diff >= 0, jnp.exp(log_g[:, None, None] * jnp.maximum(diff, 0.0)), 0.0)
    cw = jnp.exp(log_g[:, None] * (idx[None, :] + 1.0))
    uw = jnp.exp(log_g[:, None] * (L - 1.0 - idx[None, :]))
    cw = jnp.broadcast_to(cw[:, :, None], (N_HEADS, L, HEAD_W))
    uw = jnp.broadcast_to(uw[:, :, None], (N_HEADS, L, HEAD_W))
    gl = jnp.exp(log_g * L)
    rows = pl.BlockSpec((L, N_HEADS * HEAD_W), lambda c: (c, 0))
    const3 = lambda shape: pl.BlockSpec(shape, lambda c: (0, 0, 0))
    o, st = pl.pallas_call(
        _ret_prompt_body,
        grid=(t // L,),
        in_specs=[pl.BlockSpec(memory_space=pltpu.SMEM), rows, rows, rows,
                  pl.BlockSpec((L, N_HEADS * HEAD_W), lambda c: (c, 1)),
                  const3((N_HEADS, L, L)), const3((N_HEADS, L, HEAD_W)), const3((N_HEADS, L, HEAD_W))],
        out_specs=[rows, const3((N_HEADS, HEAD_W, HEAD_W))],
        out_shape=[jax.ShapeDtypeStruct((t, N_HEADS * HEAD_W), BF16),
                   jax.ShapeDtypeStruct((N_HEADS, HEAD_W, HEAD_W), F32)],
        scratch_shapes=[pltpu.VMEM((N_HEADS, HEAD_W, HEAD_W), F32)],
        compiler_params=_cparams(("arbitrary",)),
        name="ret_prompt",
    )(gl, rq, rk, rvg, rvg, decay, cw, uw)
    return o, st


def _ret_sample_body(gam_ref, q_ref, k_ref, v_ref, g_ref, s_ref, o_ref, sn_ref):
    q, k, v = q_ref[0], k_ref[0], v_ref[0]
    qb, vb = q.astype(BF16), v.astype(BF16)
    head = lax.broadcasted_iota(jnp.int32, q.shape, 0)
    qk = jnp.sum(q * k, axis=-1, keepdims=True)
    outs = []
    for h in range(N_HEADS):
        state = s_ref[0, h]
        gam = gam_ref[h]
        cross = jnp.dot(qb, state.astype(BF16), preferred_element_type=F32)[h:h + 1]
        outs.append(qk[h:h + 1] * v[h:h + 1] + gam * cross)
        k_only = jnp.where(head == h, k, 0.0).astype(BF16)
        sn_ref[0, h] = gam * state + lax.dot_general(k_only, vb, TN_DIMS, preferred_element_type=F32)
    o = jnp.concatenate(outs, axis=0)
    o_ref[0] = (_rms(o) * _silu(g_ref[0])).astype(o_ref.dtype)


def _ret_sample(rq, rk, rv, rg, state):
    b = rq.shape[0]
    gam = jnp.exp(_ret_log_decay())
    rows = lambda a: a.reshape(b, N_HEADS, HEAD_W)
    row_spec = pl.BlockSpec((1, N_HEADS, HEAD_W), lambda i: (i, 0, 0))
    st_spec = pl.BlockSpec((1, N_HEADS, HEAD_W, HEAD_W), lambda i: (i, 0, 0, 0))
    o, sn = pl.pallas_call(
        _ret_sample_body,
        grid=(b,),
        in_specs=[pl.BlockSpec(memory_space=pltpu.SMEM), row_spec, row_spec, row_spec, row_spec, st_spec],
        out_specs=[row_spec, st_spec],
        out_shape=[jax.ShapeDtypeStruct((b, N_HEADS, HEAD_W), BF16),
                   jax.ShapeDtypeStruct(state.shape, F32)],
        compiler_params=_cparams(("parallel",)),
        name="ret_sample",
    )(gam, rows(rq), rows(rk), rows(rv), rows(rg), state)
    return o.reshape(b, N_HEADS * HEAD_W), sn


def _lambda(lv_ref, lam_init):
    lv = lv_ref[...]
    e1 = jnp.exp(jnp.sum(lv[0:1] * lv[1:2], axis=-1, keepdims=True))
    e2 = jnp.exp(jnp.sum(lv[2:3] * lv[3:4], axis=-1, keepdims=True))
    return e1 - e2 + lam_init


def _diff_prompt_body(q_ref, k_ref, v_ref, lv_ref, g_ref, *rest, tq, lam_init, with_ret, n_tab):
    n_ret_in, n_ret_out = (6, 2) if with_ret else (0, 0)
    ret_in, tab_in = rest[:n_ret_in], rest[n_ret_in:n_ret_in + n_tab]
    rest = rest[n_ret_in + n_tab:]
    o_ref, ret_out, tab_out = rest[0], rest[1:1 + n_ret_out], rest[1 + n_ret_out:1 + n_ret_out + n_tab]
    acc1, acc2 = rest[1 + n_ret_out + n_tab:]
    if with_ret:
        _ret_sample_body(*ret_in, *ret_out)
    for src, dst in zip(tab_in, tab_out):
        dst[...] = src[...].astype(dst.dtype)

    qi = pl.program_id(1)
    q1, q2 = q_ref[:, :DIFF_D], q_ref[:, DIFF_D:]

    def step(kv, carry, masked):
        ks = pl.multiple_of(kv * tq, tq)
        k = k_ref[pl.ds(ks, tq), :]
        v = v_ref[pl.ds(ks, tq), :]
        m1, l1, m2, l2 = carry
        s1 = _nt(q1, k[:, :DIFF_D])
        s2 = _nt(q2, k[:, DIFF_D:])
        if masked:
            keep = (lax.broadcasted_iota(jnp.int32, (tq, tq), 1) <= lax.broadcasted_iota(jnp.int32, (tq, tq), 0))
            s1 = jnp.where(keep, s1, -jnp.inf)
            s2 = jnp.where(keep, s2, -jnp.inf)
        m1n = jnp.maximum(m1, jnp.max(s1, axis=-1, keepdims=True))
        m2n = jnp.maximum(m2, jnp.max(s2, axis=-1, keepdims=True))
        p1 = jnp.exp2(s1 - m1n)
        p2 = jnp.exp2(s2 - m2n)
        pv1 = jnp.dot(p1.astype(BF16), v, preferred_element_type=F32)
        pv2 = jnp.dot(p2.astype(BF16), v, preferred_element_type=F32)
        a1 = jnp.exp2(m1 - m1n)
        a2 = jnp.exp2(m2 - m2n)
        l1n = a1 * l1 + jnp.sum(p1, axis=-1, keepdims=True)
        l2n = a2 * l2 + jnp.sum(p2, axis=-1, keepdims=True)
        acc1[...] = a1 * acc1[...] + pv1
        acc2[...] = a2 * acc2[...] + pv2
        return m1n, l1n, m2n, l2n

    acc1[...] = jnp.zeros_like(acc1)
    acc2[...] = jnp.zeros_like(acc2)
    neg = jnp.full((tq, 1), -jnp.inf, F32)
    zero = jnp.zeros((tq, 1), F32)
    carry = lax.fori_loop(0, qi, lambda kv, c: step(kv, c, False), (neg, zero, neg, zero))
    m1, l1, m2, l2 = step(qi, carry, True)
    lam = _lambda(lv_ref, lam_init)
    o = acc1[...] / l1 - lam * (acc2[...] / l2)
    o_ref[...] = (_rms(o) * g_ref[...] * (1.0 - lam_init)).astype(o_ref.dtype)


def _diff_prompt(dq, dk, dv, lam_vecs, subln, lam_init, tq=512, ret_step=None, round_tables=()):
    t = dq.shape[0]
    tq = min(tq, t)
    nq = t // tq
    steps = N_HEADS * nq
    qspec = pl.BlockSpec((tq, HEAD_W), lambda h, i: (i, h))
    kvspec = pl.BlockSpec((t, HEAD_W), lambda h, i: (0, h))
    in_specs = [qspec, kvspec, kvspec,
                pl.BlockSpec((4, DIFF_D), lambda h, i: (0, 0)),
                pl.BlockSpec((1, HEAD_W), lambda h, i: (0, 0))]
    args = [dq, dk, dv, lam_vecs, subln.reshape(1, HEAD_W)]
    out_specs = [qspec]
    out_shape = [jax.ShapeDtypeStruct((t, N_HEADS * HEAD_W), BF16)]
    if ret_step is not None:
        rq, rk, rv, rg, state = ret_step
        assert rq.shape[0] == steps
        rows = lambda a: a.reshape(steps, N_HEADS, HEAD_W)
        row_spec = pl.BlockSpec((1, N_HEADS, HEAD_W), lambda h, i: (h * nq + i, 0, 0))
        st_spec = pl.BlockSpec((1, N_HEADS, HEAD_W, HEAD_W), lambda h, i: (h * nq + i, 0, 0, 0))
        in_specs += [pl.BlockSpec(memory_space=pltpu.SMEM), row_spec, row_spec, row_spec, row_spec, st_spec]
        args += [jnp.exp(_ret_log_decay()), rows(rq), rows(rk), rows(rv), rows(rg), state]
        out_specs += [row_spec, st_spec]
        out_shape += [jax.ShapeDtypeStruct((steps, N_HEADS, HEAD_W), BF16), jax.ShapeDtypeStruct(state.shape, F32)]
    tab_specs = []
    for tab in round_tables:
        assert tab.shape[0] % steps == 0
        tab_specs.append(pl.BlockSpec((tab.shape[0] // steps, tab.shape[1]), lambda h, i: (h * nq + i, 0)))
        args.append(tab)
        out_shape.append(jax.ShapeDtypeStruct(tab.shape, BF16))
    outs = pl.pallas_call(
        functools.partial(_diff_prompt_body, tq=tq, lam_init=lam_init, with_ret=ret_step is not None,
                          n_tab=len(round_tables)),
        grid=(N_HEADS, nq),
        in_specs=in_specs + tab_specs,
        out_specs=out_specs + tab_specs,
        out_shape=out_shape,
        scratch_shapes=[pltpu.VMEM((tq, HEAD_W), F32), pltpu.VMEM((tq, HEAD_W), F32)],
        compiler_params=_cparams(("parallel", "arbitrary")),
        name="diff_prompt",
    )(*args)
    ret_out = None
    if ret_step is not None:
        ret_out = (outs[1].reshape(steps, N_HEADS * HEAD_W), outs[2])
    return outs[0], ret_out, tuple(outs[len(outs) - len(round_tables):])


PAGES_PER_STEP = 8


def _diff_sample_body(pt_ref, q_ref, kn_ref, vn_ref, *rest, page, lam_init):
    ck_refs, cv_refs = rest[:PAGES_PER_STEP], rest[PAGES_PER_STEP:2 * PAGES_PER_STEP]
    lv_ref, g_ref, o_ref, m_sc, l_sc, acc_sc = rest[2 * PAGES_PER_STEP:]
    p = pl.program_id(1)

    @pl.when(p == 0)
    def _():
        m_sc[...] = jnp.full_like(m_sc, -jnp.inf)
        l_sc[...] = jnp.zeros_like(l_sc)
        acc_sc[...] = jnp.zeros_like(acc_sc)

    def fold(n, s):
        m_old = m_sc[n][:, :1]
        m_new = jnp.maximum(m_old, jnp.max(s, axis=-1, keepdims=True))
        alpha = jnp.exp(m_old - m_new)
        pr = jnp.exp(s - m_new)
        l_new = alpha * l_sc[n][:, :1] + jnp.sum(pr, axis=-1, keepdims=True)
        m_sc[n] = jnp.broadcast_to(m_new, m_sc.shape[1:])
        l_sc[n] = jnp.broadcast_to(l_new, l_sc.shape[1:])
        return alpha, pr

    rows = page * N_HEADS
    own_head = (lax.broadcasted_iota(jnp.int32, (N_HEADS, rows), 1) % N_HEADS
                == lax.broadcasted_iota(jnp.int32, (N_HEADS, rows), 0))
    q_both = jnp.concatenate([q_ref[0, 0], q_ref[0, 1]], axis=0).astype(BF16)
    alphas, weights = [], []
    for n in range(2):
        s_pages = []
        for ck_ref in ck_refs:
            k_rows = ck_ref[0, pl.ds(n, rows, stride=2), :].astype(BF16)
            s_pages.append(jnp.where(own_head, _nt(q_both, k_rows)[n * N_HEADS:(n + 1) * N_HEADS], -jnp.inf))
        alpha, pr = fold(n, jnp.concatenate(s_pages, axis=1))
        alphas.append(alpha)
        weights.append(pr)
    w_both = jnp.concatenate(weights, axis=0).astype(BF16)
    pv = jnp.zeros((2 * N_HEADS, HEAD_W), F32)
    for j, cv_ref in enumerate(cv_refs):
        pv += jnp.dot(w_both[:, j * rows:(j + 1) * rows], cv_ref[0].astype(BF16), preferred_element_type=F32)
    for n in range(2):
        acc_sc[n] = alphas[n] * acc_sc[n] + pv[n * N_HEADS:(n + 1) * N_HEADS]

    @pl.when(p == pl.num_programs(1) - 1)
    def _():
        outs = []
        for n in range(2):
            s = jnp.sum(kn_ref[0, n] * q_ref[0, n], axis=-1, keepdims=True)
            alpha, pr = fold(n, s)
            acc_sc[n] = alpha * acc_sc[n] + pr * vn_ref[0]
            outs.append(acc_sc[n] / l_sc[n][:, :1])
        o = outs[0] - _lambda(lv_ref, lam_init) * outs[1]
        o_ref[0] = (_rms(o) * g_ref[...] * (1.0 - lam_init)).astype(o_ref.dtype)


def _diff_sample(dq, dk, dv, cache_k, cache_v, page_table, lam_vecs, subln, lam_init):
    b, npg = page_table.shape
    n_pool, page = cache_k.shape[:2]
    pps = PAGES_PER_STEP
    assert npg % pps == 0
    to_maps = lambda a: a.reshape(b, N_HEADS, 2, DIFF_D).transpose(0, 2, 1, 3)
    ck = cache_k.reshape(n_pool, page * N_HEADS * 2, DIFF_D)
    cv = cache_v.reshape(n_pool, page * N_HEADS, HEAD_W)
    qk_spec = pl.BlockSpec((1, 2, N_HEADS, DIFF_D), lambda i, p, pt: (i, 0, 0, 0))
    hv_spec = pl.BlockSpec((1, N_HEADS, HEAD_W), lambda i, p, pt: (i, 0, 0))

    def page_spec(shape, j):
        return pl.BlockSpec((1,) + shape, lambda i, p, pt: (pt[i * npg + p * pps + j], 0, 0))

    grid_spec = pltpu.PrefetchScalarGridSpec(
        num_scalar_prefetch=1,
        grid=(b, npg // pps),
        in_specs=([qk_spec, qk_spec, hv_spec]
                  + [page_spec((page * N_HEADS * 2, DIFF_D), j) for j in range(pps)]
                  + [page_spec((page * N_HEADS, HEAD_W), j) for j in range(pps)]
                  + [pl.BlockSpec((4, DIFF_D), lambda i, p, pt: (0, 0)),
                     pl.BlockSpec((1, HEAD_W), lambda i, p, pt: (0, 0))]),
        out_specs=hv_spec,
        scratch_shapes=[pltpu.VMEM((2, N_HEADS, LANES), F32), pltpu.VMEM((2, N_HEADS, LANES), F32),
                        pltpu.VMEM((2, N_HEADS, HEAD_W), F32)],
    )
    o = pl.pallas_call(
        functools.partial(_diff_sample_body, page=page, lam_init=lam_init),
        grid_spec=grid_spec,
        out_shape=jax.ShapeDtypeStruct((b, N_HEADS, HEAD_W), BF16),
        compiler_params=_cparams(("parallel", "arbitrary")),
        name="diff_sample",
    )(page_table.reshape(-1), to_maps(dq), to_maps(dk), dv.reshape(b, N_HEADS, HEAD_W),
      *([ck] * pps), *([cv] * pps), lam_vecs, subln.reshape(1, HEAD_W))
    return o.reshape(b, N_HEADS * HEAD_W)


W_PITCH = PEER_NKEYS + 8


def _topk_rows(x, k, ids=None):
    if ids is None:
        ids = lax.broadcasted_iota(jnp.int32, x.shape, 0).astype(F32)
    n = x.shape[0]
    assert n % 8 == 0
    id_slabs = [ids[r:r + 8] for r in range(0, n, 8)]
    vals, idxs = [], []
    for _ in range(k):
        v_l, i_l = [x[r:r + 8] for r in range(0, n, 8)], id_slabs
        while len(v_l) > 1:
            v_n, i_n = [], []
            for a in range(0, len(v_l) - 1, 2):
                first = v_l[a] >= v_l[a + 1]
                v_n.append(jnp.where(first, v_l[a], v_l[a + 1]))
                i_n.append(jnp.where(first, i_l[a], i_l[a + 1]))
            if len(v_l) % 2:
                v_n.append(v_l[-1])
                i_n.append(i_l[-1])
            v_l, i_l = v_n, i_n
        m = jnp.max(v_l[0], axis=0, keepdims=True)
        idx = jnp.min(jnp.where(v_l[0] == m, i_l[0], 1e9), axis=0, keepdims=True)
        x = jnp.where(ids == idx, -jnp.inf, x)
        vals.append(m)
        idxs.append(idx)
    return jnp.concatenate(vals, axis=0), jnp.concatenate(idxs, axis=0)


def _pair_candidates(s1, s2):
    k, t = s1.shape
    assert k == 16
    blocks = [s1[0:1] + s2] + [s1[i:i + 1] + s2[0:8] for i in range(1, 8)] + [s1[8:16] + s2[0:1]]
    r = lax.broadcasted_iota(jnp.int32, (80, t), 0)
    mid = (((r - 16) >> 3) + 1) * k + ((r - 16) & 7)
    ids = jnp.where(r < 16, r, jnp.where(r < 72, mid, (r - 64) * k))
    return jnp.concatenate(blocks, axis=0), ids.astype(F32)


def _pick_rows(table, sel):
    out = jnp.zeros(sel.shape, F32)
    for r in range(table.shape[0]):
        out = jnp.where(sel == float(r), table[r:r + 1], out)
    return out


def _peer_route_body(q_ref, keys_ref, w_ref, a_sc, b_sc, g_sc, wbuf, *, tm):
    qb = q_ref[...].astype(BF16)
    a_rows, b_rows, g_rows = [], [], []
    for h in range(N_HEADS):
        halves = []
        for n in range(2):
            hn = 2 * h + n
            s_t = _nt(keys_ref[hn], qb[:, hn * LANES:(hn + 1) * LANES])
            halves.append(_topk_rows(s_t, PEER_TOPK))
        (s1, i1), (s2, i2) = halves
        cand, cand_ids = _pair_candidates(s1, s2)
        top, sel = _topk_rows(cand, PEER_TOPK, cand_ids)
        sel_i = jnp.floor(sel * (1.0 / PEER_TOPK))
        sel_j = sel - sel_i * PEER_TOPK
        a_rows.append(_pick_rows(i1, sel_i))
        b_rows.append(_pick_rows(i2, sel_j))
        e = jnp.exp(top - top[0:1])
        g_rows.append(e / jnp.sum(e, axis=0, keepdims=True))
    a_sc[...] = jnp.concatenate(a_rows, axis=0).T
    b_sc[...] = jnp.concatenate(b_rows, axis=0).T
    g_sc[...] = jnp.concatenate(g_rows, axis=0).T

    sub = lax.broadcasted_iota(jnp.int32, (PEER_NKEYS, N_HEADS * PEER_TOPK), 0).astype(F32)

    def token(t, carry):
        ar, br, gr = a_sc[pl.ds(t, 1), :], b_sc[pl.ds(t, 1), :], g_sc[pl.ds(t, 1), :]
        p_mat = jnp.where(sub == ar, 1.0, 0.0).astype(BF16)
        q_mat = jnp.where(sub == br, gr, 0.0).astype(BF16)
        wbuf[pl.ds(pl.multiple_of(t * W_PITCH, 8), PEER_NKEYS), :] = _nt(p_mat, q_mat)
        return carry

    lax.fori_loop(0, tm, token, 0, unroll=32)
    for a in range(PEER_NKEYS):
        w_ref[a] = wbuf[pl.ds(a, tm, stride=W_PITCH), :].astype(w_ref.dtype)


def _peer_route(q, keys, tm=128):
    m = q.shape[0]
    tm = min(tm, m)
    slots = N_HEADS * PEER_TOPK
    return pl.pallas_call(
        functools.partial(_peer_route_body, tm=tm),
        grid=(m // tm,),
        in_specs=[pl.BlockSpec((tm, q.shape[1]), lambda i: (i, 0)),
                  pl.BlockSpec(keys.shape, lambda i: (0, 0, 0))],
        out_specs=pl.BlockSpec((PEER_NKEYS, tm, PEER_NKEYS), lambda i: (0, i, 0)),
        out_shape=jax.ShapeDtypeStruct((PEER_NKEYS, m, PEER_NKEYS), BF16),
        scratch_shapes=[pltpu.VMEM((tm, slots), F32), pltpu.VMEM((tm, slots), F32), pltpu.VMEM((tm, slots), F32),
                        pltpu.VMEM((tm * W_PITCH, PEER_NKEYS), F32)],
        compiler_params=_cparams(("parallel",)),
        name="peer_route",
    )(q, keys)


def _peer_dense_body(h_ref, u_ref, v_ref, w_ref, o_ref, *, ta, td):
    n = pl.program_id(1)

    @pl.when(n == 0)
    def _():
        o_ref[...] = jnp.zeros_like(o_ref)

    s = _nt(h_ref[...], u_ref[...])
    act = 0.5 * s * (1.0 + lax.erf(s * INV_SQRT2))
    c = jnp.concatenate([act[:, a * LANES:(a + 1) * LANES] * w_ref[a].astype(F32) for a in range(ta)], axis=1)
    c = c.astype(BF16)
    for d0 in range(0, o_ref.shape[1], td):
        o_ref[:, d0:d0 + td] += jnp.dot(c, v_ref[:, d0:d0 + td], preferred_element_type=F32)


def _peer_dense(hn, u, v, w, tm=512, tn=1024, td=1024):
    m, d = hn.shape
    n_exp = v.shape[0]
    tm = min(tm, m)
    ta = tn // LANES
    once = pl.Buffered(1)
    return pl.pallas_call(
        functools.partial(_peer_dense_body, ta=ta, td=td),
        grid=(m // tm, n_exp // tn),
        in_specs=[pl.BlockSpec((tm, d), lambda i, n: (i, 0), pipeline_mode=once),
                  pl.BlockSpec((tn, d), lambda i, n: (n, 0)),
                  pl.BlockSpec((tn, d), lambda i, n: (n, 0)),
                  pl.BlockSpec((ta, tm, LANES), lambda i, n: (n, i, 0))],
        out_specs=pl.BlockSpec((tm, d), lambda i, n: (i, 0), pipeline_mode=once),
        out_shape=jax.ShapeDtypeStruct((m, d), F32),
        compiler_params=_cparams(("parallel", "arbitrary")),
        name="peer_dense",
    )(hn, u, v, w)


def _rope_tables(pos, d, tiled=False):
    inv = ROPE_THETA ** (-jnp.arange(0, d, 2, dtype=F32) / d)
    if tiled:
        inv = jnp.concatenate([inv, inv])
    ang = pos.astype(F32)[:, None] * inv[None, :]
    if tiled:
        sign = jnp.concatenate([-jnp.ones((d // 2,), F32), jnp.ones((d // 2,), F32)])
        return jnp.cos(ang), jnp.sin(ang) * sign[None, :]
    return jnp.cos(ang), jnp.sin(ang)


def _in_proj(x, pos, is_prompt, norm_mix, w_in):
    seg = N_HEADS * HEAD_W
    act_dt = BF16 if is_prompt else F32
    xn = _rmsnorm(x, norm_mix, BF16)
    c256, s256 = _rope_tables(pos, HEAD_W)
    cc, ss = _rope_tables(pos, DIFF_D, tiled=True)
    rope_r = ((c256, "row"), (s256, "row"))
    rope_d = ((cc, "row"), (ss, "row"))

    def seg_proj(i, n, epi, dts, **kw):
        return _proj(xn, w_in, i * seg, n * seg, epi, dts, name=f"in_proj_{i}", **kw)

    p = {}
    (p["rq"],) = seg_proj(0, 1, "rope256", [act_dt], extras=rope_r)
    (p["rk"],) = seg_proj(1, 1, "rope256", [act_dt], extras=rope_r, scale=HEAD_W ** -0.5)
    (p["rvg"],) = seg_proj(2, 2, "plain", [act_dt])
    q_scale = DIFF_D ** -0.5 * (LOG2E if is_prompt else 1.0)
    (p["dq"],) = seg_proj(4, 1, "rope128", [act_dt], extras=rope_d, scale=q_scale)
    p["dk_f"], p["dk_b"] = seg_proj(5, 1, "rope128", [F32, BF16], extras=rope_d)
    p["dv_f"], p["dv_b"] = seg_proj(6, 1, "plain", [F32, BF16])
    (p["gates"],) = seg_proj(7, 4, "sigmoid", [BF16])
    return p


def _tail(x, p, ret_o, diff_o, w_ret_out, w_diff_out, w_o, norm_ffn, peer_w_q, peer_keys, peer_u, peer_v,
          norm_final):
    d = x.shape[1]
    merged = _merge(ret_o, diff_o, w_ret_out, w_diff_out, p["gates"])
    (x1,) = _proj(merged, w_o, 0, d, "residual", [F32], extras=((x, "tile"),), name="out_proj")
    hn = _rmsnorm(x1, norm_ffn, BF16)
    (pq,) = _proj(hn, peer_w_q, 0, peer_w_q.shape[1], "plain", [F32], name="peer_q")
    w = _peer_route(pq, peer_keys)
    return _add_rmsnorm(x1, _peer_dense(hn, peer_u, peer_v, w), norm_final)


def kernel(x_prompt, x_sample, cache_k, cache_v, state_ret, page_table, norm_mix, w_in, lambda_q1, lambda_k1,
           lambda_q2, lambda_k2, diff_subln, w_ret_out, w_diff_out, w_o, norm_ffn, peer_w_q, peer_keys, peer_u,
           peer_v, norm_final):
    depth = w_in.shape[0]
    assert depth == 1
    b_p, t_p, d = x_prompt.shape
    b_s, t_s, _ = x_sample.shape
    assert b_p == 1 and t_s == 1
    past = page_table.shape[1] * cache_k.shape[2]
    seg = N_HEADS * HEAD_W
    l = 0
    lam_init = 0.8 - 0.6 * math.exp(-0.3 * l)
    lam_vecs = jnp.stack([lambda_q1[l], lambda_k1[l], lambda_q2[l], lambda_k2[l]])
    keys = peer_keys[l].reshape(N_HEADS * 2, PEER_NKEYS, -1).astype(BF16)
    xp = x_prompt.reshape(t_p, d)
    xs = x_sample.reshape(b_s, d)

    pp = _in_proj(xp, jnp.arange(t_p), True, norm_mix[l], w_in[l])
    ps = _in_proj(xs, jnp.full((b_s,), past), False, norm_mix[l], w_in[l])

    ret_o_p, state_p = _ret_prompt(pp["rq"], pp["rk"], pp["rvg"])
    diff_o_p, (ret_o_s, state_s), (u_bf, v_bf) = _diff_prompt(
        pp["dq"], pp["dk_b"], pp["dv_b"], lam_vecs, diff_subln[l], lam_init,
        ret_step=(ps["rq"], ps["rk"], ps["rvg"][:, :seg], ps["rvg"][:, seg:], state_ret[l]),
        round_tables=(peer_u[l], peer_v[l]))
    diff_o_s = _diff_sample(ps["dq"], ps["dk_f"], ps["dv_f"], cache_k[l], cache_v[l], page_table, lam_vecs,
                            diff_subln[l], lam_init)

    shared = (w_ret_out[l], w_diff_out[l], w_o[l], norm_ffn[l], peer_w_q[l], keys, u_bf, v_bf, norm_final)
    y_prompt = _tail(xp, pp, ret_o_p, diff_o_p, *shared).reshape(b_p, t_p, d)
    y_sample = _tail(xs, ps, ret_o_s, diff_o_s, *shared).reshape(b_s, t_s, d)
    k_prompt = pp["dk_f"].reshape(1, b_p, t_p, N_HEADS, 2, DIFF_D)
    v_prompt = pp["dv_f"].reshape(1, b_p, t_p, N_HEADS, HEAD_W)
    k_sample = ps["dk_f"].reshape(1, b_s, t_s, N_HEADS, 2, DIFF_D)
    v_sample = ps["dv_f"].reshape(1, b_s, t_s, N_HEADS, HEAD_W)
    return (y_prompt, y_sample, k_prompt, v_prompt, state_p[None, None], k_sample, v_sample, state_s[None])
```

```python
import functools
import math

import jax
import jax.numpy as jnp
import numpy as np
from jax import lax
from jax.experimental import pallas as pl
from jax.experimental.pallas import tpu as pltpu

F32 = jnp.float32
BF16 = jnp.bfloat16

LANES = 128
VMEM_LIMIT = 56 * 1024 * 1024

EPS = 1e-6
ROPE_THETA = 10000.0
N_HEADS = 8
HEAD_W = 256
DIFF_D = 128
PEER_NKEYS = 128
PEER_TOPK = 16
RET_L = 256
INV_SQRT2 = 0.7071067811865476
LOG2E = 1.4426950408889634

NT_DIMS = (((1,), (1,)), ((), ()))
TN_DIMS = (((0,), (0,)), ((), ()))


def _cparams(sem):
    return pltpu.CompilerParams(dimension_semantics=sem, vmem_limit_bytes=VMEM_LIMIT)


def _nt(a, b):
    return lax.dot_general(a, b, NT_DIMS, preferred_element_type=F32)


def _rms(x):
    return x * lax.rsqrt(jnp.mean(x * x, axis=-1, keepdims=True) + EPS)


def _rmsnorm_body(x_ref, g_ref, o_ref):
    x = x_ref[...]
    o_ref[...] = (_rms(x) * g_ref[...]).astype(o_ref.dtype)


def _rmsnorm(x, g, out_dtype, tm=256):
    m, d = x.shape
    tm = min(tm, m)
    return pl.pallas_call(
        _rmsnorm_body,
        grid=(m // tm,),
        in_specs=[pl.BlockSpec((tm, d), lambda i: (i, 0)), pl.BlockSpec((1, d), lambda i: (0, 0))],
        out_specs=pl.BlockSpec((tm, d), lambda i: (i, 0)),
        out_shape=jax.ShapeDtypeStruct((m, d), out_dtype),
        compiler_params=_cparams(("parallel",)),
        name="rmsnorm",
    )(x, g.reshape(1, d))


def _add_rmsnorm_body(x_ref, y_ref, g_ref, o_ref):
    o_ref[...] = (_rms(x_ref[...] + y_ref[...]) * g_ref[...]).astype(o_ref.dtype)


def _add_rmsnorm(x, y, g, tm=256):
    m, d = x.shape
    tm = min(tm, m)
    row = pl.BlockSpec((tm, d), lambda i: (i, 0))
    return pl.pallas_call(
        _add_rmsnorm_body,
        grid=(m // tm,),
        in_specs=[row, row, pl.BlockSpec((1, d), lambda i: (0, 0))],
        out_specs=row,
        out_shape=jax.ShapeDtypeStruct((m, d), F32),
        compiler_params=_cparams(("parallel",)),
        name="add_rmsnorm",
    )(x, y, g.reshape(1, d))


def _epilogue(acc, epi, scale, extras, tn):
    if epi == "plain":
        return acc
    if epi == "sigmoid":
        return jax.nn.sigmoid(acc)
    if epi == "residual":
        return acc + extras[0][...]
    if epi == "rope256":
        c, s = extras[0][...], extras[1][...]
        parts = []
        for b in range(tn // HEAD_W):
            x1 = acc[:, b * HEAD_W:b * HEAD_W + LANES]
            x2 = acc[:, b * HEAD_W + LANES:(b + 1) * HEAD_W]
            parts += [x1 * c - x2 * s, x2 * c + x1 * s]
        return jnp.concatenate(parts, axis=1) * scale
    if epi == "rope128":
        cc, ss = extras[0][...], extras[1][...]
        parts = []
        for b in range(tn // LANES):
            xb = acc[:, b * LANES:(b + 1) * LANES]
            parts.append(xb * cc + pltpu.roll(xb, LANES // 2, 1) * ss)
        return jnp.concatenate(parts, axis=1) * scale
    raise ValueError(epi)


def _proj_body(x_ref, w_ref, *rest, epi, scales, n_extra, n_outs, tn):
    has_small = len(n_outs) == 2
    w_bf = rest[-1]
    extras, pos = rest[:n_extra], n_extra
    if has_small:
        xs_ref, extras_s = rest[pos], rest[pos + 1:pos + 1 + n_extra]
        pos += 1 + n_extra
    outs = rest[pos:pos + n_outs[0]]
    outs_s = rest[pos + n_outs[0]:-1]

    @pl.when(pl.program_id(1) == 0)
    def _():
        w_bf[...] = w_ref[...].astype(BF16)
        if has_small:
            acc_s = jnp.dot(xs_ref[...], w_bf[...], preferred_element_type=F32)
            res_s = _epilogue(acc_s, epi, scales[1], extras_s, tn)
            for o in outs_s:
                o[...] = res_s.astype(o.dtype)

    acc = jnp.dot(x_ref[...], w_bf[...], preferred_element_type=F32)
    res = _epilogue(acc, epi, scales[0], extras, tn)
    for o in outs:
        o[...] = res.astype(o.dtype)


def _proj(x, w, col0, ncols, epi, out_dtypes, extras=(), scale=1.0, tm=1024, tn=512, name="proj", small=None):
    m, k = x.shape
    tm = min(tm, m)
    assert m % tm == 0 and ncols % tn == 0 and col0 % tn == 0
    j0 = col0 // tn
    in_specs = [pl.BlockSpec((tm, k), lambda j, i: (i, 0)),
                pl.BlockSpec((k, tn), lambda j, i: (0, j + j0))]
    args = [x, w]
    for arr, kind in extras:
        if kind == "row":
            in_specs.append(pl.BlockSpec((tm, arr.shape[1]), lambda j, i: (i, 0)))
        else:
            in_specs.append(pl.BlockSpec((tm, tn), lambda j, i: (i, j)))
        args.append(arr)
    out_specs = [pl.BlockSpec((tm, tn), lambda j, i: (i, j))] * len(out_dtypes)
    out_shape = [jax.ShapeDtypeStruct((m, ncols), dt) for dt in out_dtypes]
    scales, n_outs = (scale,), (len(out_dtypes),)
    if small is not None:
        x2, extras2, scale2, out_dtypes2 = small
        m2 = x2.shape[0]
        assert len(extras2) == len(extras)
        in_specs.append(pl.BlockSpec((m2, k), lambda j, i: (0, 0)))
        args.append(x2)
        for arr, kind in extras2:
            if kind == "row":
                in_specs.append(pl.BlockSpec((m2, arr.shape[1]), lambda j, i: (0, 0)))
            else:
                in_specs.append(pl.BlockSpec((m2, tn), lambda j, i: (0, j)))
            args.append(arr)
        out_specs += [pl.BlockSpec((m2, tn), lambda j, i: (0, j))] * len(out_dtypes2)
        out_shape += [jax.ShapeDtypeStruct((m2, ncols), dt) for dt in out_dtypes2]
        scales, n_outs = (scale, scale2), (len(out_dtypes), len(out_dtypes2))
    outs = pl.pallas_call(
        functools.partial(_proj_body, epi=epi, scales=scales, n_extra=len(extras), n_outs=n_outs, tn=tn),
        grid=(ncols // tn, m // tm),
        in_specs=in_specs,
        out_specs=out_specs,
        out_shape=out_shape,
        scratch_shapes=[pltpu.VMEM((k, tn), BF16)],
        compiler_params=_cparams(("parallel", "arbitrary")),
        name=name,
    )(*args)
    return outs


def _merge_body(r_ref, d_ref, wr_ref, wd_ref, ga_ref, gb_ref, *rest):
    has_small = len(rest) == 8
    wr_bf, wd_bf = rest[-2:]
    o_ref = rest[4] if has_small else rest[0]

    def gated(r, d, ga, gb):
        a = jnp.dot(r[...], wr_bf[...], preferred_element_type=F32)
        b = jnp.dot(d[...], wd_bf[...], preferred_element_type=F32)
        return ga[...].astype(F32) * a + gb[...].astype(F32) * b

    @pl.when(pl.program_id(1) == 0)
    def _():
        wr_bf[...] = wr_ref[...].astype(BF16)
        wd_bf[...] = wd_ref[...].astype(BF16)
        if has_small:
            rest[5][...] = gated(*rest[:4]).astype(rest[5].dtype)

    o_ref[...] = gated(r_ref, d_ref, ga_ref, gb_ref).astype(o_ref.dtype)


def _merge(ret_o, diff_o, w_ret, w_diff, gates, tm=1024, tn=512, small=None):
    m, k = ret_o.shape
    n = w_ret.shape[1]
    tm = min(tm, m)
    nj = n // tn
    row = pl.BlockSpec((tm, k), lambda j, i: (i, 0))
    wcol = pl.BlockSpec((k, tn), lambda j, i: (0, j))
    tile = pl.BlockSpec((tm, tn), lambda j, i: (i, j))
    in_specs = [row, row, wcol, wcol, tile, pl.BlockSpec((tm, tn), lambda j, i: (i, j + nj))]
    args = [ret_o, diff_o, w_ret, w_diff, gates, gates]
    out_specs, out_shape = [tile], [jax.ShapeDtypeStruct((m, n), BF16)]
    if small is not None:
        r2, d2, g2 = small
        m2 = r2.shape[0]
        row2 = pl.BlockSpec((m2, k), lambda j, i: (0, 0))
        in_specs += [row2, row2, pl.BlockSpec((m2, tn), lambda j, i: (0, j)),
                     pl.BlockSpec((m2, tn), lambda j, i: (0, j + nj))]
        args += [r2, d2, g2, g2]
        out_specs.append(pl.BlockSpec((m2, tn), lambda j, i: (0, j)))
        out_shape.append(jax.ShapeDtypeStruct((m2, n), BF16))
    outs = pl.pallas_call(
        _merge_body,
        grid=(nj, m // tm),
        in_specs=in_specs,
        out_specs=out_specs,
        out_shape=out_shape,
        scratch_shapes=[pltpu.VMEM((k, tn), BF16), pltpu.VMEM((k, tn), BF16)],
        compiler_params=_cparams(("parallel", "arbitrary")),
        name="merge",
    )(*args)
    return outs


def _ret_log_decay():
    return jnp.log(1.0 - jnp.exp2(-5.0 - jnp.arange(N_HEADS, dtype=F32)))


def _silu(g):
    return g * jax.nn.sigmoid(g)


def _ret_prompt_body(gl_ref, q_ref, k_ref, v_ref, g_ref, dec_ref, cw_ref, uw_ref, o_ref, st_ref, s_sc):
    c = pl.program_id(0)

    @pl.when(c == 0)
    def _():
        s_sc[...] = jnp.zeros_like(s_sc)

    for h in range(N_HEADS):
        sl = slice(h * HEAD_W, (h + 1) * HEAD_W)
        q, k, v = q_ref[:, sl], k_ref[:, sl], v_ref[:, sl]
        state = s_sc[h]
        scores = _nt(q, k) * dec_ref[h]
        o = jnp.dot(scores.astype(BF16), v, preferred_element_type=F32)
        o = o + jnp.dot(q, state.astype(BF16), preferred_element_type=F32) * cw_ref[h]
        kw = (k.astype(F32) * uw_ref[h]).astype(BF16)
        s_sc[h] = gl_ref[h] * state + lax.dot_general(kw, v, TN_DIMS, preferred_element_type=F32)
        o_ref[:, sl] = (_rms(o) * _silu(g_ref[:, sl].astype(F32))).astype(o_ref.dtype)

    @pl.when(c == pl.num_programs(0) - 1)
    def _():
        st_ref[...] = s_sc[...]


def _ret_prompt(rq, rk, rvg):
    t = rq.shape[0]
    L = RET_L
    log_g = _ret_log_decay()
    idx = jnp.arange(L, dtype=F32)
    diff = idx[:, None] - idx[None, :]
    decay = jnp.where(diff >= 0, jnp.exp(log_g[:, None, None] * jnp.maximum(diff, 0.0)), 0.0)
    cw = jnp.exp(log_g[:, None] * (idx[None, :] + 1.0))
    uw = jnp.exp(log_g[:, None] * (L - 1.0 - idx[None, :]))
    cw = jnp.broadcast_to(cw[:, :, None], (N_HEADS, L, HEAD_W))
    uw = jnp.broadcast_to(uw[:, :, None], (N_HEADS, L, HEAD_W))
    gl = jnp.exp(log_g * L)
    rows = pl.BlockSpec((L, N_HEADS * HEAD_W), lambda c: (c, 0))
    const3 = lambda shape: pl.BlockSpec(shape, lambda c: (0, 0, 0))
    o, st = pl.pallas_call(
        _ret_prompt_body,
        grid=(t // L,),
        in_specs=[pl.BlockSpec(memory_space=pltpu.SMEM), rows, rows, rows,
                  pl.BlockSpec((L, N_HEADS * HEAD_W), lambda c: (c, 1)),
                  const3((N_HEADS, L, L)), const3((N_HEADS, L, HEAD_W)), const3((N_HEADS, L, HEAD_W))],
        out_specs=[rows, const3((N_HEADS, HEAD_W, HEAD_W))],
        out_shape=[jax.ShapeDtypeStruct((t, N_HEADS * HEAD_W), BF16),
                   jax.ShapeDtypeStruct((N_HEADS, HEAD_W, HEAD_W), F32)],
        scratch_shapes=[pltpu.VMEM((N_HEADS, HEAD_W, HEAD_W), F32)],
        compiler_params=_cparams(("arbitrary",)),
        name="ret_prompt",
    )(gl, rq, rk, rvg, rvg, decay, cw, uw)
    return o, st


def _ret_sample_body(gam_ref, q_ref, k_ref, v_ref, g_ref, s_ref, o_ref, sn_ref):
    q, k, v = q_ref[0], k_ref[0], v_ref[0]
    qb, vb = q.astype(BF16), v.astype(BF16)
    head = lax.broadcasted_iota(jnp.int32, q.shape, 0)
    qk = jnp.sum(q * k, axis=-1, keepdims=True)
    outs = []
    for h in range(N_HEADS):
        state = s_ref[0, h]
        gam = gam_ref[h]
        cross = jnp.dot(qb, state.astype(BF16), preferred_element_type=F32)[h:h + 1]
        outs.append(qk[h:h + 1] * v[h:h + 1] + gam * cross)
        k_only = jnp.where(head == h, k, 0.0).astype(BF16)
        sn_ref[0, h] = gam * state + lax.dot_general(k_only, vb, TN_DIMS, preferred_element_type=F32)
    o = jnp.concatenate(outs, axis=0)
    o_ref[0] = (_rms(o) * _silu(g_ref[0])).astype(o_ref.dtype)


def _ret_sample(rq, rk, rv, rg, state):
    b = rq.shape[0]
    gam = jnp.exp(_ret_log_decay())
    rows = lambda a: a.reshape(b, N_HEADS, HEAD_W)
    row_spec = pl.BlockSpec((1, N_HEADS, HEAD_W), lambda i: (i, 0, 0))
    st_spec = pl.BlockSpec((1, N_HEADS, HEAD_W, HEAD_W), lambda i: (i, 0, 0, 0))
    o, sn = pl.pallas_call(
        _ret_sample_body,
        grid=(b,),
        in_specs=[pl.BlockSpec(memory_space=pltpu.SMEM), row_spec, row_spec, row_spec, row_spec, st_spec],
        out_specs=[row_spec, st_spec],
        out_shape=[jax.ShapeDtypeStruct((b, N_HEADS, HEAD_W), BF16),
                   jax.ShapeDtypeStruct(state.shape, F32)],
        compiler_params=_cparams(("parallel",)),
        name="ret_sample",
    )(gam, rows(rq), rows(rk), rows(rv), rows(rg), state)
    return o.reshape(b, N_HEADS * HEAD_W), sn


def _lambda(lv_ref, lam_init):
    lv = lv_ref[...]
    e1 = jnp.exp(jnp.sum(lv[0:1] * lv[1:2], axis=-1, keepdims=True))
    e2 = jnp.exp(jnp.sum(lv[2:3] * lv[3:4], axis=-1, keepdims=True))
    return e1 - e2 + lam_init


def _diff_prompt_body(q_ref, k_ref, v_ref, lv_ref, g_ref, *rest, tq, lam_init, with_ret, n_tab):
    n_ret_in, n_ret_out = (6, 2) if with_ret else (0, 0)
    ret_in, tab_in = rest[:n_ret_in], rest[n_ret_in:n_ret_in + n_tab]
    rest = rest[n_ret_in + n_tab:]
    o_ref, ret_out, tab_out = rest[0], rest[1:1 + n_ret_out], rest[1 + n_ret_out:1 + n_ret_out + n_tab]
    acc1, acc2 = rest[1 + n_ret_out + n_tab:]
    if with_ret:
        _ret_sample_body(*ret_in, *ret_out)
    for src, dst in zip(tab_in, tab_out):
        dst[...] = src[...].astype(dst.dtype)

    qi = pl.program_id(1)
    q1, q2 = q_ref[:, :DIFF_D], q_ref[:, DIFF_D:]

    def step(kv, carry, masked):
        ks = pl.multiple_of(kv * tq, tq)
        k = k_ref[pl.ds(ks, tq), :]
        v = v_ref[pl.ds(ks, tq), :]
        m1, l1, m2, l2 = carry
        s1 = _nt(q1, k[:, :DIFF_D])
        s2 = _nt(q2, k[:, DIFF_D:])
        if masked:
            keep = (lax.broadcasted_iota(jnp.int32, (tq, tq), 1) <= lax.broadcasted_iota(jnp.int32, (tq, tq), 0))
            s1 = jnp.where(keep, s1, -jnp.inf)
            s2 = jnp.where(keep, s2, -jnp.inf)
        m1n = jnp.maximum(m1, jnp.max(s1, axis=-1, keepdims=True))
        m2n = jnp.maximum(m2, jnp.max(s2, axis=-1, keepdims=True))
        p1 = jnp.exp2(s1 - m1n)
        p2 = jnp.exp2(s2 - m2n)
        pv1 = jnp.dot(p1.astype(BF16), v, preferred_element_type=F32)
        pv2 = jnp.dot(p2.astype(BF16), v, preferred_element_type=F32)
        a1 = jnp.exp2(m1 - m1n)
        a2 = jnp.exp2(m2 - m2n)
        l1n = a1 * l1 + jnp.sum(p1, axis=-1, keepdims=True)
        l2n = a2 * l2 + jnp.sum(p2, axis=-1, keepdims=True)
        acc1[...] = a1 * acc1[...] + pv1
        acc2[...] = a2 * acc2[...] + pv2
        return m1n, l1n, m2n, l2n

    acc1[...] = jnp.zeros_like(acc1)
    acc2[...] = jnp.zeros_like(acc2)
    neg = jnp.full((tq, 1), -jnp.inf, F32)
    zero = jnp.zeros((tq, 1), F32)
    carry = lax.fori_loop(0, qi, lambda kv, c: step(kv, c, False), (neg, zero, neg, zero))
    m1, l1, m2, l2 = step(qi, carry, True)
    lam = _lambda(lv_ref, lam_init)
    o = acc1[...] / l1 - lam * (acc2[...] / l2)
    o_ref[...] = (_rms(o) * g_ref[...] * (1.0 - lam_init)).astype(o_ref.dtype)


def _diff_prompt(dq, dk, dv, lam_vecs, subln, lam_init, tq=512, ret_step=None, round_tables=()):
    t = dq.shape[0]
    tq = min(tq, t)
    nq = t // tq
    steps = N_HEADS * nq
    qspec = pl.BlockSpec((tq, HEAD_W), lambda h, i: (i, h))
    kvspec = pl.BlockSpec((t, HEAD_W), lambda h, i: (0, h))
    in_specs = [qspec, kvspec, kvspec,
                pl.BlockSpec((4, DIFF_D), lambda h, i: (0, 0)),
                pl.BlockSpec((1, HEAD_W), lambda h, i: (0, 0))]
    args = [dq, dk, dv, lam_vecs, subln.reshape(1, HEAD_W)]
    out_specs = [qspec]
    out_shape = [jax.ShapeDtypeStruct((t, N_HEADS * HEAD_W), BF16)]
    if ret_step is not None:
        rq, rk, rv, rg, state = ret_step
        assert rq.shape[0] == steps
        rows = lambda a: a.reshape(steps, N_HEADS, HEAD_W)
        row_spec = pl.BlockSpec((1, N_HEADS, HEAD_W), lambda h, i: (h * nq + i, 0, 0))
        st_spec = pl.BlockSpec((1, N_HEADS, HEAD_W, HEAD_W), lambda h, i: (h * nq + i, 0, 0, 0))
        in_specs += [pl.BlockSpec(memory_space=pltpu.SMEM), row_spec, row_spec, row_spec, row_spec, st_spec]
        args += [jnp.exp(_ret_log_decay()), rows(rq), rows(rk), rows(rv), rows(rg), state]
        out_specs += [row_spec, st_spec]
        out_shape += [jax.ShapeDtypeStruct((steps, N_HEADS, HEAD_W), BF16), jax.ShapeDtypeStruct(state.shape, F32)]
    tab_specs = []
    for tab in round_tables:
        assert tab.shape[0] % steps == 0
        tab_specs.append(pl.BlockSpec((tab.shape[0] // steps, tab.shape[1]), lambda h, i: (h * nq + i, 0)))
        args.append(tab)
        out_shape.append(jax.ShapeDtypeStruct(tab.shape, BF16))
    outs = pl.pallas_call(
        functools.partial(_diff_prompt_body, tq=tq, lam_init=lam_init, with_ret=ret_step is not None,
                          n_tab=len(round_tables)),
        grid=(N_HEADS, nq),
        in_specs=in_specs + tab_specs,
        out_specs=out_specs + tab_specs,
        out_shape=out_shape,
        scratch_shapes=[pltpu.VMEM((tq, HEAD_W), F32), pltpu.VMEM((tq, HEAD_W), F32)],
        compiler_params=_cparams(("parallel", "arbitrary")),
        name="diff_prompt",
    )(*args)
    ret_out = None
    if ret_step is not None:
        ret_out = (outs[1].reshape(steps, N_HEADS * HEAD_W), outs[2])
    return outs[0], ret_out, tuple(outs[len(outs) - len(round_tables):])


PAGES_PER_STEP = 8


def _diff_sample_body(pt_ref, q_ref, kn_ref, vn_ref, *rest, page, lam_init):
    ck_refs, cv_refs = rest[:PAGES_PER_STEP], rest[PAGES_PER_STEP:2 * PAGES_PER_STEP]
    lv_ref, g_ref, o_ref, m_sc, l_sc, acc_sc = rest[2 * PAGES_PER_STEP:]
    p = pl.program_id(1)

    @pl.when(p == 0)
    def _():
        m_sc[...] = jnp.full_like(m_sc, -jnp.inf)
        l_sc[...] = jnp.zeros_like(l_sc)
        acc_sc[...] = jnp.zeros_like(acc_sc)

    def fold(n, s):
        m_old = m_sc[n][:, :1]
        m_new = jnp.maximum(m_old, jnp.max(s, axis=-1, keepdims=True))
        alpha = jnp.exp(m_old - m_new)
        pr = jnp.exp(s - m_new)
        l_new = alpha * l_sc[n][:, :1] + jnp.sum(pr, axis=-1, keepdims=True)
        m_sc[n] = jnp.broadcast_to(m_new, m_sc.shape[1:])
        l_sc[n] = jnp.broadcast_to(l_new, l_sc.shape[1:])
        return alpha, pr

    rows = page * N_HEADS
    own_head = (lax.broadcasted_iota(jnp.int32, (N_HEADS, rows), 1) % N_HEADS
                == lax.broadcasted_iota(jnp.int32, (N_HEADS, rows), 0))
    q_both = jnp.concatenate([q_ref[0, 0], q_ref[0, 1]], axis=0).astype(BF16)
    alphas, weights = [], []
    for n in range(2):
        s_pages = []
        for ck_ref in ck_refs:
            k_rows = ck_ref[0, pl.ds(n, rows, stride=2), :].astype(BF16)
            s_pages.append(jnp.where(own_head, _nt(q_both, k_rows)[n * N_HEADS:(n + 1) * N_HEADS], -jnp.inf))
        alpha, pr = fold(n, jnp.concatenate(s_pages, axis=1))
        alphas.append(alpha)
        weights.append(pr)
    w_both = jnp.concatenate(weights, axis=0).astype(BF16)
    pv = jnp.zeros((2 * N_HEADS, HEAD_W), F32)
    for j, cv_ref in enumerate(cv_refs):
        pv += jnp.dot(w_both[:, j * rows:(j + 1) * rows], cv_ref[0].astype(BF16), preferred_element_type=F32)
    for n in range(2):
        acc_sc[n] = alphas[n] * acc_sc[n] + pv[n * N_HEADS:(n + 1) * N_HEADS]

    @pl.when(p == pl.num_programs(1) - 1)
    def _():
        outs = []
        for n in range(2):
            s = jnp.sum(kn_ref[0, n] * q_ref[0, n], axis=-1, keepdims=True)
            alpha, pr = fold(n, s)
            acc_sc[n] = alpha * acc_sc[n] + pr * vn_ref[0]
            outs.append(acc_sc[n] / l_sc[n][:, :1])
        o = outs[0] - _lambda(lv_ref, lam_init) * outs[1]
        o_ref[0] = (_rms(o) * g_ref[...] * (1.0 - lam_init)).astype(o_ref.dtype)


def _diff_sample(dq, dk, dv, cache_k, cache_v, page_table, lam_vecs, subln, lam_init):
    b, npg = page_table.shape
    n_pool, page = cache_k.shape[:2]
    pps = PAGES_PER_STEP
    assert npg % pps == 0
    to_maps = lambda a: a.reshape(b, N_HEADS, 2, DIFF_D).transpose(0, 2, 1, 3)
    ck = cache_k.reshape(n_pool, page * N_HEADS * 2, DIFF_D)
    cv = cache_v.reshape(n_pool, page * N_HEADS, HEAD_W)
    qk_spec = pl.BlockSpec((1, 2, N_HEADS, DIFF_D), lambda i, p, pt: (i, 0, 0, 0))
    hv_spec = pl.BlockSpec((1, N_HEADS, HEAD_W), lambda i, p, pt: (i, 0, 0))

    def page_spec(shape, j):
        return pl.BlockSpec((1,) + shape, lambda i, p, pt: (pt[i * npg + p * pps + j], 0, 0))

    grid_spec = pltpu.PrefetchScalarGridSpec(
        num_scalar_prefetch=1,
        grid=(b, npg // pps),
        in_specs=([qk_spec, qk_spec, hv_spec]
                  + [page_spec((page * N_HEADS * 2, DIFF_D), j) for j in range(pps)]
                  + [page_spec((page * N_HEADS, HEAD_W), j) for j in range(pps)]
                  + [pl.BlockSpec((4, DIFF_D), lambda i, p, pt: (0, 0)),
                     pl.BlockSpec((1, HEAD_W), lambda i, p, pt: (0, 0))]),
        out_specs=hv_spec,
        scratch_shapes=[pltpu.VMEM((2, N_HEADS, LANES), F32), pltpu.VMEM((2, N_HEADS, LANES), F32),
                        pltpu.VMEM((2, N_HEADS, HEAD_W), F32)],
    )
    o = pl.pallas_call(
        functools.partial(_diff_sample_body, page=page, lam_init=lam_init),
        grid_spec=grid_spec,
        out_shape=jax.ShapeDtypeStruct((b, N_HEADS, HEAD_W), BF16),
        compiler_params=_cparams(("parallel", "arbitrary")),
        name="diff_sample",
    )(page_table.reshape(-1), to_maps(dq), to_maps(dk), dv.reshape(b, N_HEADS, HEAD_W),
      *([ck] * pps), *([cv] * pps), lam_vecs, subln.reshape(1, HEAD_W))
    return o.reshape(b, N_HEADS * HEAD_W)


W_PITCH = PEER_NKEYS + 8


def _topk_rows(x, k, ids=None):
    if ids is None:
        ids = lax.broadcasted_iota(jnp.int32, x.shape, 0).astype(F32)
    n = x.shape[0]
    assert n % 8 == 0
    id_slabs = [ids[r:r + 8] for r in range(0, n, 8)]
    vals, idxs = [], []
    for _ in range(k):
        v_l, i_l = [x[r:r + 8] for r in range(0, n, 8)], id_slabs
        while len(v_l) > 1:
            v_n, i_n = [], []
            for a in range(0, len(v_l) - 1, 2):
                first = v_l[a] >= v_l[a + 1]
                v_n.append(jnp.where(first, v_l[a], v_l[a + 1]))
                i_n.append(jnp.where(first, i_l[a], i_l[a + 1]))
            if len(v_l) % 2:
                v_n.append(v_l[-1])
                i_n.append(i_l[-1])
            v_l, i_l = v_n, i_n
        m = jnp.max(v_l[0], axis=0, keepdims=True)
        idx = jnp.min(jnp.where(v_l[0] == m, i_l[0], 1e9), axis=0, keepdims=True)
        x = jnp.where(ids == idx, -jnp.inf, x)
        vals.append(m)
        idxs.append(idx)
    return jnp.concatenate(vals, axis=0), jnp.concatenate(idxs, axis=0)


def _pair_candidates(s1, s2):
    k, t = s1.shape
    assert k == 16
    blocks = [s1[0:1] + s2] + [s1[i:i + 1] + s2[0:8] for i in range(1, 8)] + [s1[8:16] + s2[0:1]]
    r = lax.broadcasted_iota(jnp.int32, (80, t), 0)
    mid = (((r - 16) >> 3) + 1) * k + ((r - 16) & 7)
    ids = jnp.where(r < 16, r, jnp.where(r < 72, mid, (r - 64) * k))
    return jnp.concatenate(blocks, axis=0), ids.astype(F32)


def _pick_rows(table, sel):
    out = jnp.zeros(sel.shape, F32)
    for r in range(table.shape[0]):
        out = jnp.where(sel == float(r), table[r:r + 1], out)
    return out


def _peer_route_body(q_ref, keys_ref, w_ref, a_sc, b_sc, g_sc, wbuf, *, tm):
    qb = q_ref[...].astype(BF16)
    a_rows, b_rows, g_rows = [], [], []
    for h in range(N_HEADS):
        halves = []
        for n in range(2):
            hn = 2 * h + n
            s_t = _nt(keys_ref[hn], qb[:, hn * LANES:(hn + 1) * LANES])
            halves.append(_topk_rows(s_t, PEER_TOPK))
        (s1, i1), (s2, i2) = halves
        cand, cand_ids = _pair_candidates(s1, s2)
        top, sel = _topk_rows(cand, PEER_TOPK, cand_ids)
        sel_i = jnp.floor(sel * (1.0 / PEER_TOPK))
        sel_j = sel - sel_i * PEER_TOPK
        a_rows.append(_pick_rows(i1, sel_i))
        b_rows.append(_pick_rows(i2, sel_j))
        e = jnp.exp(top - top[0:1])
        g_rows.append(e / jnp.sum(e, axis=0, keepdims=True))
    a_sc[...] = jnp.concatenate(a_rows, axis=0).T
    b_sc[...] = jnp.concatenate(b_rows, axis=0).T
    g_sc[...] = jnp.concatenate(g_rows, axis=0).T

    sub = lax.broadcasted_iota(jnp.int32, (PEER_NKEYS, N_HEADS * PEER_TOPK), 0).astype(F32)

    def token(t, carry):
        ar, br, gr = a_sc[pl.ds(t, 1), :], b_sc[pl.ds(t, 1), :], g_sc[pl.ds(t, 1), :]
        p_mat = jnp.where(sub == ar, 1.0, 0.0).astype(BF16)
        q_mat = jnp.where(sub == br, gr, 0.0).astype(BF16)
        wbuf[pl.ds(pl.multiple_of(t * W_PITCH, 8), PEER_NKEYS), :] = _nt(p_mat, q_mat)
        return carry

    lax.fori_loop(0, tm, token, 0, unroll=32)
    for a in range(PEER_NKEYS):
        w_ref[a] = wbuf[pl.ds(a, tm, stride=W_PITCH), :].astype(w_ref.dtype)


def _peer_route(q, keys, tm=128):
    m = q.shape[0]
    tm = min(tm, m)
    slots = N_HEADS * PEER_TOPK
    return pl.pallas_call(
        functools.partial(_peer_route_body, tm=tm),
        grid=(m // tm,),
        in_specs=[pl.BlockSpec((tm, q.shape[1]), lambda i: (i, 0)),
                  pl.BlockSpec(keys.shape, lambda i: (0, 0, 0))],
        out_specs=pl.BlockSpec((PEER_NKEYS, tm, PEER_NKEYS), lambda i: (0, i, 0)),
        out_shape=jax.ShapeDtypeStruct((PEER_NKEYS, m, PEER_NKEYS), BF16),
        scratch_shapes=[pltpu.VMEM((tm, slots), F32), pltpu.VMEM((tm, slots), F32), pltpu.VMEM((tm, slots), F32),
                        pltpu.VMEM((tm * W_PITCH, PEER_NKEYS), F32)],
        compiler_params=_cparams(("parallel",)),
        name="peer_route",
    )(q, keys)


def _peer_dense_body(h_ref, u_ref, v_ref, w_ref, o_ref, *, ta, td):
    n = pl.program_id(1)

    @pl.when(n == 0)
    def _():
        o_ref[...] = jnp.zeros_like(o_ref)

    s = _nt(h_ref[...], u_ref[...])
    act = 0.5 * s * (1.0 + lax.erf(s * INV_SQRT2))
    c = jnp.concatenate([act[:, a * LANES:(a + 1) * LANES] * w_ref[a].astype(F32) for a in range(ta)], axis=1)
    c = c.astype(BF16)
    for d0 in range(0, o_ref.shape[1], td):
        o_ref[:, d0:d0 + td] += jnp.dot(c, v_ref[:, d0:d0 + td], preferred_element_type=F32)


def _peer_dense(hn, u, v, w, tm=512, tn=1024, td=1024):
    m, d = hn.shape
    n_exp = v.shape[0]
    tm = min(tm, m)
    ta = tn // LANES
    once = pl.Buffered(1)
    return pl.pallas_call(
        functools.partial(_peer_dense_body, ta=ta, td=td),
        grid=(m // tm, n_exp // tn),
        in_specs=[pl.BlockSpec((tm, d), lambda i, n: (i, 0), pipeline_mode=once),
                  pl.BlockSpec((tn, d), lambda i, n: (n, 0)),
                  pl.BlockSpec((tn, d), lambda i, n: (n, 0)),
                  pl.BlockSpec((ta, tm, LANES), lambda i, n: (n, i, 0))],
        out_specs=pl.BlockSpec((tm, d), lambda i, n: (i, 0), pipeline_mode=once),
        out_shape=jax.ShapeDtypeStruct((m, d), F32),
        compiler_params=_cparams(("parallel", "arbitrary")),
        name="peer_dense",
    )(hn, u, v, w)


def _rope_tables(pos, d, tiled=False):
    inv = ROPE_THETA ** (-jnp.arange(0, d, 2, dtype=F32) / d)
    if tiled:
        inv = jnp.concatenate([inv, inv])
    ang = pos.astype(F32)[:, None] * inv[None, :]
    if tiled:
        sign = jnp.concatenate([-jnp.ones((d // 2,), F32), jnp.ones((d // 2,), F32)])
        return jnp.cos(ang), jnp.sin(ang) * sign[None, :]
    return jnp.cos(ang), jnp.sin(ang)


def _in_proj(xp, xs, pos_p, pos_s, norm_mix, w_in):
    seg = N_HEADS * HEAD_W
    xn_p, xn_s = _rmsnorm(xp, norm_mix, BF16), _rmsnorm(xs, norm_mix, BF16)

    def tables(pos):
        c256, s256 = _rope_tables(pos, HEAD_W)
        cc, ss = _rope_tables(pos, DIFF_D, tiled=True)
        return ((c256, "row"), (s256, "row")), ((cc, "row"), (ss, "row"))

    (rope_r_p, rope_d_p), (rope_r_s, rope_d_s) = tables(pos_p), tables(pos_s)

    def seg_proj(i, n, epi, dts_p, dts_s, extras=((), ()), scales=(1.0, 1.0)):
        outs = _proj(xn_p, w_in, i * seg, n * seg, epi, dts_p, extras=extras[0], scale=scales[0],
                     name=f"in_proj_{i}", small=(xn_s, extras[1], scales[1], dts_s))
        return outs[:len(dts_p)], outs[len(dts_p):]

    pp, ps = {}, {}
    (pp["rq"],), (ps["rq"],) = seg_proj(0, 1, "rope256", [BF16], [F32], (rope_r_p, rope_r_s))
    (pp["rk"],), (ps["rk"],) = seg_proj(1, 1, "rope256", [BF16], [F32], (rope_r_p, rope_r_s),
                                        (HEAD_W ** -0.5, HEAD_W ** -0.5))
    (pp["rvg"],), (ps["rvg"],) = seg_proj(2, 2, "plain", [BF16], [F32])
    (pp["dq"],), (ps["dq"],) = seg_proj(4, 1, "rope128", [BF16], [F32], (rope_d_p, rope_d_s),
                                        (DIFF_D ** -0.5 * LOG2E, DIFF_D ** -0.5))
    (pp["dk_f"], pp["dk_b"]), (ps["dk_f"],) = seg_proj(5, 1, "rope128", [F32, BF16], [F32], (rope_d_p, rope_d_s))
    (pp["dv_f"], pp["dv_b"]), (ps["dv_f"],) = seg_proj(6, 1, "plain", [F32, BF16], [F32])
    (pp["gates"],), (ps["gates"],) = seg_proj(7, 4, "sigmoid", [BF16], [BF16])
    return pp, ps


def _tail(xp, xs, pp, ps, mix_p, mix_s, w_ret_out, w_diff_out, w_o, norm_ffn, peer_w_q, peer_keys, peer_u, peer_v,
          norm_final):
    d = xp.shape[1]
    merged_p, merged_s = _merge(*mix_p, w_ret_out, w_diff_out, pp["gates"], small=(*mix_s, ps["gates"]))
    x1_p, x1_s = _proj(merged_p, w_o, 0, d, "residual", [F32], extras=((xp, "tile"),), name="out_proj",
                       small=(merged_s, ((xs, "tile"),), 1.0, [F32]))
    hn_p, hn_s = _rmsnorm(x1_p, norm_ffn, BF16), _rmsnorm(x1_s, norm_ffn, BF16)
    pq_p, pq_s = _proj(hn_p, peer_w_q, 0, peer_w_q.shape[1], "plain", [F32], name="peer_q",
                       small=(hn_s, (), 1.0, [F32]))
    ys = []
    for x1, hn, pq in ((x1_p, hn_p, pq_p), (x1_s, hn_s, pq_s)):
        w = _peer_route(pq, peer_keys)
        ys.append(_add_rmsnorm(x1, _peer_dense(hn, peer_u, peer_v, w), norm_final))
    return ys


def kernel(x_prompt, x_sample, cache_k, cache_v, state_ret, page_table, norm_mix, w_in, lambda_q1, lambda_k1,
           lambda_q2, lambda_k2, diff_subln, w_ret_out, w_diff_out, w_o, norm_ffn, peer_w_q, peer_keys, peer_u,
           peer_v, norm_final):
    depth = w_in.shape[0]
    assert depth == 1
    b_p, t_p, d = x_prompt.shape
    b_s, t_s, _ = x_sample.shape
    assert b_p == 1 and t_s == 1
    past = page_table.shape[1] * cache_k.shape[2]
    seg = N_HEADS * HEAD_W
    l = 0
    lam_init = 0.8 - 0.6 * math.exp(-0.3 * l)
    lam_vecs = jnp.stack([lambda_q1[l], lambda_k1[l], lambda_q2[l], lambda_k2[l]])
    keys = peer_keys[l].reshape(N_HEADS * 2, PEER_NKEYS, -1).astype(BF16)
    xp = x_prompt.reshape(t_p, d)
    xs = x_sample.reshape(b_s, d)

    pp, ps = _in_proj(xp, xs, jnp.arange(t_p), jnp.full((b_s,), past), norm_mix[l], w_in[l])

    ret_o_p, state_p = _ret_prompt(pp["rq"], pp["rk"], pp["rvg"])
    diff_o_p, (ret_o_s, state_s), (u_bf, v_bf) = _diff_prompt(
        pp["dq"], pp["dk_b"], pp["dv_b"], lam_vecs, diff_subln[l], lam_init,
        ret_step=(ps["rq"], ps["rk"], ps["rvg"][:, :seg], ps["rvg"][:, seg:], state_ret[l]),
        round_tables=(peer_u[l], peer_v[l]))
    diff_o_s = _diff_sample(ps["dq"], ps["dk_f"], ps["dv_f"], cache_k[l], cache_v[l], page_table, lam_vecs,
                            diff_subln[l], lam_init)

    y_p, y_s = _tail(xp, xs, pp, ps, (ret_o_p, diff_o_p), (ret_o_s, diff_o_s), w_ret_out[l], w_diff_out[l], w_o[l],
                     norm_ffn[l], peer_w_q[l], keys, u_bf, v_bf, norm_final)
    k_prompt = pp["dk_f"].reshape(1, b_p, t_p, N_HEADS, 2, DIFF_D)
    v_prompt = pp["dv_f"].reshape(1, b_p, t_p, N_HEADS, HEAD_W)
    k_sample = ps["dk_f"].reshape(1, b_s, t_s, N_HEADS, 2, DIFF_D)
    v_sample = ps["dv_f"].reshape(1, b_s, t_s, N_HEADS, HEAD_W)
    return (y_p.reshape(b_p, t_p, d), y_s.reshape(b_s, t_s, d), k_prompt, v_prompt, state_p[None, None],
            k_sample, v_sample, state_s[None])
```

```python
import functools
import math

import jax
import jax.numpy as jnp
import numpy as np
from jax import lax
from jax.experimental import pallas as pl
from jax.experimental.pallas import tpu as pltpu

F32 = jnp.float32
BF16 = jnp.bfloat16

LANES = 128
VMEM_LIMIT = 56 * 1024 * 1024

EPS = 1e-6
ROPE_THETA = 10000.0
N_HEADS = 8
HEAD_W = 256
DIFF_D = 128
PEER_NKEYS = 128
PEER_TOPK = 16
RET_L = 256
INV_SQRT2 = 0.7071067811865476
LOG2E = 1.4426950408889634

NT_DIMS = (((1,), (1,)), ((), ()))
TN_DIMS = (((0,), (0,)), ((), ()))


def _cparams(sem):
    return pltpu.CompilerParams(dimension_semantics=sem, vmem_limit_bytes=VMEM_LIMIT)


def _nt(a, b):
    return lax.dot_general(a, b, NT_DIMS, preferred_element_type=F32)


def _rms(x):
    return x * lax.rsqrt(jnp.mean(x * x, axis=-1, keepdims=True) + EPS)


def _rmsnorm_body(x_ref, g_ref, o_ref):
    x = x_ref[...]
    o_ref[...] = (_rms(x) * g_ref[...]).astype(o_ref.dtype)


def _rmsnorm(x, g, out_dtype, tm=256):
    m, d = x.shape
    tm = min(tm, m)
    return pl.pallas_call(
        _rmsnorm_body,
        grid=(m // tm,),
        in_specs=[pl.BlockSpec((tm, d), lambda i: (i, 0)), pl.BlockSpec((1, d), lambda i: (0, 0))],
        out_specs=pl.BlockSpec((tm, d), lambda i: (i, 0)),
        out_shape=jax.ShapeDtypeStruct((m, d), out_dtype),
        compiler_params=_cparams(("parallel",)),
        name="rmsnorm",
    )(x, g.reshape(1, d))


def _add_rmsnorm_body(x_ref, y_ref, g_ref, o_ref):
    o_ref[...] = (_rms(x_ref[...] + y_ref[...]) * g_ref[...]).astype(o_ref.dtype)


def _add_rmsnorm(x, y, g, tm=256):
    m, d = x.shape
    tm = min(tm, m)
    row = pl.BlockSpec((tm, d), lambda i: (i, 0))
    return pl.pallas_call(
        _add_rmsnorm_body,
        grid=(m // tm,),
        in_specs=[row, row, pl.BlockSpec((1, d), lambda i: (0, 0))],
        out_specs=row,
        out_shape=jax.ShapeDtypeStruct((m, d), F32),
        compiler_params=_cparams(("parallel",)),
        name="add_rmsnorm",
    )(x, y, g.reshape(1, d))


def _epilogue(acc, epi, scale, extras, tn):
    if epi == "plain":
        return acc
    if epi == "sigmoid":
        return jax.nn.sigmoid(acc)
    if epi == "residual":
        return acc + extras[0][...]
    if epi == "rope256":
        c, s = extras[0][...], extras[1][...]
        parts = []
        for b in range(tn // HEAD_W):
            x1 = acc[:, b * HEAD_W:b * HEAD_W + LANES]
            x2 = acc[:, b * HEAD_W + LANES:(b + 1) * HEAD_W]
            parts += [x1 * c - x2 * s, x2 * c + x1 * s]
        return jnp.concatenate(parts, axis=1) * scale
    if epi == "rope128":
        cc, ss = extras[0][...], extras[1][...]
        parts = []
        for b in range(tn // LANES):
            xb = acc[:, b * LANES:(b + 1) * LANES]
            parts.append(xb * cc + pltpu.roll(xb, LANES // 2, 1) * ss)
        return jnp.concatenate(parts, axis=1) * scale
    raise ValueError(epi)


def _proj_body(x_ref, w_ref, *rest, epi, scales, n_extra, n_outs, tn):
    has_small = len(n_outs) == 2
    w_bf = rest[-1]
    extras, pos = rest[:n_extra], n_extra
    if has_small:
        xs_ref, extras_s = rest[pos], rest[pos + 1:pos + 1 + n_extra]
        pos += 1 + n_extra
    outs = rest[pos:pos + n_outs[0]]
    outs_s = rest[pos + n_outs[0]:-1]

    @pl.when(pl.program_id(1) == 0)
    def _():
        w_bf[...] = w_ref[...].astype(BF16)
        if has_small:
            acc_s = jnp.dot(xs_ref[...], w_bf[...], preferred_element_type=F32)
            res_s = _epilogue(acc_s, epi, scales[1], extras_s, tn)
            for o in outs_s:
                o[...] = res_s.astype(o.dtype)

    acc = jnp.dot(x_ref[...], w_bf[...], preferred_element_type=F32)
    res = _epilogue(acc, epi, scales[0], extras, tn)
    for o in outs:
        o[...] = res.astype(o.dtype)


def _proj(x, w, col0, ncols, epi, out_dtypes, extras=(), scale=1.0, tm=1024, tn=512, name="proj", small=None):
    m, k = x.shape
    tm = min(tm, m)
    assert m % tm == 0 and ncols % tn == 0 and col0 % tn == 0
    j0 = col0 // tn
    in_specs = [pl.BlockSpec((tm, k), lambda j, i: (i, 0)),
                pl.BlockSpec((k, tn), lambda j, i: (0, j + j0))]
    args = [x, w]
    for arr, kind in extras:
        if kind == "row":
            in_specs.append(pl.BlockSpec((tm, arr.shape[1]), lambda j, i: (i, 0)))
        else:
            in_specs.append(pl.BlockSpec((tm, tn), lambda j, i: (i, j)))
        args.append(arr)
    out_specs = [pl.BlockSpec((tm, tn), lambda j, i: (i, j))] * len(out_dtypes)
    out_shape = [jax.ShapeDtypeStruct((m, ncols), dt) for dt in out_dtypes]
    scales, n_outs = (scale,), (len(out_dtypes),)
    if small is not None:
        x2, extras2, scale2, out_dtypes2 = small
        m2 = x2.shape[0]
        assert len(extras2) == len(extras)
        in_specs.append(pl.BlockSpec((m2, k), lambda j, i: (0, 0)))
        args.append(x2)
        for arr, kind in extras2:
            if kind == "row":
                in_specs.append(pl.BlockSpec((m2, arr.shape[1]), lambda j, i: (0, 0)))
            else:
                in_specs.append(pl.BlockSpec((m2, tn), lambda j, i: (0, j)))
            args.append(arr)
        out_specs += [pl.BlockSpec((m2, tn), lambda j, i: (0, j))] * len(out_dtypes2)
        out_shape += [jax.ShapeDtypeStruct((m2, ncols), dt) for dt in out_dtypes2]
        scales, n_outs = (scale, scale2), (len(out_dtypes), len(out_dtypes2))
    outs = pl.pallas_call(
        functools.partial(_proj_body, epi=epi, scales=scales, n_extra=len(extras), n_outs=n_outs, tn=tn),
        grid=(ncols // tn, m // tm),
        in_specs=in_specs,
        out_specs=out_specs,
        out_shape=out_shape,
        scratch_shapes=[pltpu.VMEM((k, tn), BF16)],
        compiler_params=_cparams(("parallel", "arbitrary")),
        name=name,
    )(*args)
    return outs


def _merge_body(r_ref, d_ref, wr_ref, wd_ref, ga_ref, gb_ref, *rest):
    has_small = len(rest) == 8
    wr_bf, wd_bf = rest[-2:]
    o_ref = rest[4] if has_small else rest[0]

    def gated(r, d, ga, gb):
        a = jnp.dot(r[...], wr_bf[...], preferred_element_type=F32)
        b = jnp.dot(d[...], wd_bf[...], preferred_element_type=F32)
        return ga[...].astype(F32) * a + gb[...].astype(F32) * b

    @pl.when(pl.program_id(1) == 0)
    def _():
        wr_bf[...] = wr_ref[...].astype(BF16)
        wd_bf[...] = wd_ref[...].astype(BF16)
        if has_small:
            rest[5][...] = gated(*rest[:4]).astype(rest[5].dtype)

    o_ref[...] = gated(r_ref, d_ref, ga_ref, gb_ref).astype(o_ref.dtype)


def _merge(ret_o, diff_o, w_ret, w_diff, gates, tm=1024, tn=512, small=None):
    m, k = ret_o.shape
    n = w_ret.shape[1]
    tm = min(tm, m)
    nj = n // tn
    row = pl.BlockSpec((tm, k), lambda j, i: (i, 0))
    wcol = pl.BlockSpec((k, tn), lambda j, i: (0, j))
    tile = pl.BlockSpec((tm, tn), lambda j, i: (i, j))
    in_specs = [row, row, wcol, wcol, tile, pl.BlockSpec((tm, tn), lambda j, i: (i, j + nj))]
    args = [ret_o, diff_o, w_ret, w_diff, gates, gates]
    out_specs, out_shape = [tile], [jax.ShapeDtypeStruct((m, n), BF16)]
    if small is not None:
        r2, d2, g2 = small
        m2 = r2.shape[0]
        row2 = pl.BlockSpec((m2, k), lambda j, i: (0, 0))
        in_specs += [row2, row2, pl.BlockSpec((m2, tn), lambda j, i: (0, j)),
                     pl.BlockSpec((m2, tn), lambda j, i: (0, j + nj))]
        args += [r2, d2, g2, g2]
        out_specs.append(pl.BlockSpec((m2, tn), lambda j, i: (0, j)))
        out_shape.append(jax.ShapeDtypeStruct((m2, n), BF16))
    outs = pl.pallas_call(
        _merge_body,
        grid=(nj, m // tm),
        in_specs=in_specs,
        out_specs=out_specs,
        out_shape=out_shape,
        scratch_shapes=[pltpu.VMEM((k, tn), BF16), pltpu.VMEM((k, tn), BF16)],
        compiler_params=_cparams(("parallel", "arbitrary")),
        name="merge",
    )(*args)
    return outs


def _ret_log_decay():
    return jnp.log(1.0 - jnp.exp2(-5.0 - jnp.arange(N_HEADS, dtype=F32)))


def _silu(g):
    return g * jax.nn.sigmoid(g)


def _ret_prompt_body(gl_ref, q_ref, k_ref, v_ref, g_ref, dec_ref, cw_ref, uw_ref, o_ref, st_ref, s_sc):
    c = pl.program_id(0)

    @pl.when(c == 0)
    def _():
        s_sc[...] = jnp.zeros_like(s_sc)

    for h in range(N_HEADS):
        sl = slice(h * HEAD_W, (h + 1) * HEAD_W)
        q, k, v = q_ref[:, sl], k_ref[:, sl], v_ref[:, sl]
        state = s_sc[h]
        scores = _nt(q, k) * dec_ref[h]
        o = jnp.dot(scores.astype(BF16), v, preferred_element_type=F32)
        o = o + jnp.dot(q, state.astype(BF16), preferred_element_type=F32) * cw_ref[h]
        kw = (k.astype(F32) * uw_ref[h]).astype(BF16)
        s_sc[h] = gl_ref[h] * state + lax.dot_general(kw, v, TN_DIMS, preferred_element_type=F32)
        o_ref[:, sl] = (_rms(o) * _silu(g_ref[:, sl].astype(F32))).astype(o_ref.dtype)

    @pl.when(c == pl.num_programs(0) - 1)
    def _():
        st_ref[...] = s_sc[...]


def _ret_prompt(rq, rk, rvg):
    t = rq.shape[0]
    L = RET_L
    log_g = _ret_log_decay()
    idx = jnp.arange(L, dtype=F32)
    diff = idx[:, None] - idx[None, :]
    decay = jnp.where(diff >= 0, jnp.exp(log_g[:, None, None] * jnp.maximum(diff, 0.0)), 0.0)
    cw = jnp.exp(log_g[:, None] * (idx[None, :] + 1.0))
    uw = jnp.exp(log_g[:, None] * (L - 1.0 - idx[None, :]))
    cw = jnp.broadcast_to(cw[:, :, None], (N_HEADS, L, HEAD_W))
    uw = jnp.broadcast_to(uw[:, :, None], (N_HEADS, L, HEAD_W))
    gl = jnp.exp(log_g * L)
    rows = pl.BlockSpec((L, N_HEADS * HEAD_W), lambda c: (c, 0))
    const3 = lambda shape: pl.BlockSpec(shape, lambda c: (0, 0, 0))
    o, st = pl.pallas_call(
        _ret_prompt_body,
        grid=(t // L,),
        in_specs=[pl.BlockSpec(memory_space=pltpu.SMEM), rows, rows, rows,
                  pl.BlockSpec((L, N_HEADS * HEAD_W), lambda c: (c, 1)),
                  const3((N_HEADS, L, L)), const3((N_HEADS, L, HEAD_W)), const3((N_HEADS, L, HEAD_W))],
        out_specs=[rows, const3((N_HEADS, HEAD_W, HEAD_W))],
        out_shape=[jax.ShapeDtypeStruct((t, N_HEADS * HEAD_W), BF16),
                   jax.ShapeDtypeStruct((N_HEADS, HEAD_W, HEAD_W), F32)],
        scratch_shapes=[pltpu.VMEM((N_HEADS, HEAD_W, HEAD_W), F32)],
        compiler_params=_cparams(("arbitrary",)),
        name="ret_prompt",
    )(gl, rq, rk, rvg, rvg, decay, cw, uw)
    return o, st


def _ret_sample_body(gam_ref, q_ref, k_ref, v_ref, g_ref, s_ref, o_ref, sn_ref):
    q, k, v = q_ref[0], k_ref[0], v_ref[0]
    qb, vb = q.astype(BF16), v.astype(BF16)
    head = lax.broadcasted_iota(jnp.int32, q.shape, 0)
    qk = jnp.sum(q * k, axis=-1, keepdims=True)
    outs = []
    for h in range(N_HEADS):
        state = s_ref[0, h]
        gam = gam_ref[h]
        cross = jnp.dot(qb, state.astype(BF16), preferred_element_type=F32)[h:h + 1]
        outs.append(qk[h:h + 1] * v[h:h + 1] + gam * cross)
        k_only = jnp.where(head == h, k, 0.0).astype(BF16)
        sn_ref[0, h] = gam * state + lax.dot_general(k_only, vb, TN_DIMS, preferred_element_type=F32)
    o = jnp.concatenate(outs, axis=0)
    o_ref[0] = (_rms(o) * _silu(g_ref[0])).astype(o_ref.dtype)


def _ret_sample(rq, rk, rv, rg, state):
    b = rq.shape[0]
    gam = jnp.exp(_ret_log_decay())
    rows = lambda a: a.reshape(b, N_HEADS, HEAD_W)
    row_spec = pl.BlockSpec((1, N_HEADS, HEAD_W), lambda i: (i, 0, 0))
    st_spec = pl.BlockSpec((1, N_HEADS, HEAD_W, HEAD_W), lambda i: (i, 0, 0, 0))
    o, sn = pl.pallas_call(
        _ret_sample_body,
        grid=(b,),
        in_specs=[pl.BlockSpec(memory_space=pltpu.SMEM), row_spec, row_spec, row_spec, row_spec, st_spec],
        out_specs=[row_spec, st_spec],
        out_shape=[jax.ShapeDtypeStruct((b, N_HEADS, HEAD_W), BF16),
                   jax.ShapeDtypeStruct(state.shape, F32)],
        compiler_params=_cparams(("parallel",)),
        name="ret_sample",
    )(gam, rows(rq), rows(rk), rows(rv), rows(rg), state)
    return o.reshape(b, N_HEADS * HEAD_W), sn


def _lambda(lv_ref, lam_init):
    lv = lv_ref[...]
    e1 = jnp.exp(jnp.sum(lv[0:1] * lv[1:2], axis=-1, keepdims=True))
    e2 = jnp.exp(jnp.sum(lv[2:3] * lv[3:4], axis=-1, keepdims=True))
    return e1 - e2 + lam_init


def _diff_prompt_body(q_ref, k_ref, v_ref, lv_ref, g_ref, *rest, tq, lam_init, with_ret, n_tab):
    n_ret_in, n_ret_out = (6, 2) if with_ret else (0, 0)
    ret_in, tab_in = rest[:n_ret_in], rest[n_ret_in:n_ret_in + n_tab]
    rest = rest[n_ret_in + n_tab:]
    o_ref, ret_out, tab_out = rest[0], rest[1:1 + n_ret_out], rest[1 + n_ret_out:1 + n_ret_out + n_tab]
    acc1, acc2 = rest[1 + n_ret_out + n_tab:]
    qi = pl.program_id(1)
    q1, q2 = q_ref[:, :DIFF_D], q_ref[:, DIFF_D:]

    def step(kv, carry, masked):
        ks = pl.multiple_of(kv * tq, tq)
        k = k_ref[pl.ds(ks, tq), :]
        v = v_ref[pl.ds(ks, tq), :]
        m1, l1, m2, l2 = carry
        s1 = _nt(q1, k[:, :DIFF_D])
        s2 = _nt(q2, k[:, DIFF_D:])
        if masked:
            keep = (lax.broadcasted_iota(jnp.int32, (tq, tq), 1) <= lax.broadcasted_iota(jnp.int32, (tq, tq), 0))
            s1 = jnp.where(keep, s1, -jnp.inf)
            s2 = jnp.where(keep, s2, -jnp.inf)
        m1n = jnp.maximum(m1, jnp.max(s1, axis=-1, keepdims=True))
        m2n = jnp.maximum(m2, jnp.max(s2, axis=-1, keepdims=True))
        p1 = jnp.exp2(s1 - m1n)
        p2 = jnp.exp2(s2 - m2n)
        pv1 = jnp.dot(p1.astype(BF16), v, preferred_element_type=F32)
        pv2 = jnp.dot(p2.astype(BF16), v, preferred_element_type=F32)
        a1 = jnp.exp2(m1 - m1n)
        a2 = jnp.exp2(m2 - m2n)
        l1n = a1 * l1 + jnp.sum(p1, axis=-1, keepdims=True)
        l2n = a2 * l2 + jnp.sum(p2, axis=-1, keepdims=True)
        acc1[...] = a1 * acc1[...] + pv1
        acc2[...] = a2 * acc2[...] + pv2
        return m1n, l1n, m2n, l2n

    acc1[...] = jnp.zeros_like(acc1)
    acc2[...] = jnp.zeros_like(acc2)
    neg = jnp.full((tq, 1), -jnp.inf, F32)
    zero = jnp.zeros((tq, 1), F32)
    carry = lax.fori_loop(0, qi, lambda kv, c: step(kv, c, False), (neg, zero, neg, zero))
    if with_ret:
        _ret_sample_body(*ret_in, *ret_out)
    for src, dst in zip(tab_in, tab_out):
        dst[...] = src[...].astype(dst.dtype)
    m1, l1, m2, l2 = step(qi, carry, True)
    lam = _lambda(lv_ref, lam_init)
    o = acc1[...] / l1 - lam * (acc2[...] / l2)
    o_ref[...] = (_rms(o) * g_ref[...] * (1.0 - lam_init)).astype(o_ref.dtype)


def _diff_prompt(dq, dk, dv, lam_vecs, subln, lam_init, tq=512, ret_step=None, round_tables=()):
    t = dq.shape[0]
    tq = min(tq, t)
    nq = t // tq
    steps = N_HEADS * nq
    qspec = pl.BlockSpec((tq, HEAD_W), lambda h, i: (i, h))
    kvspec = pl.BlockSpec((t, HEAD_W), lambda h, i: (0, h))
    in_specs = [qspec, kvspec, kvspec,
                pl.BlockSpec((4, DIFF_D), lambda h, i: (0, 0)),
                pl.BlockSpec((1, HEAD_W), lambda h, i: (0, 0))]
    args = [dq, dk, dv, lam_vecs, subln.reshape(1, HEAD_W)]
    out_specs = [qspec]
    out_shape = [jax.ShapeDtypeStruct((t, N_HEADS * HEAD_W), BF16)]
    if ret_step is not None:
        rq, rk, rv, rg, state = ret_step
        assert rq.shape[0] == steps
        rows = lambda a: a.reshape(steps, N_HEADS, HEAD_W)
        row_spec = pl.BlockSpec((1, N_HEADS, HEAD_W), lambda h, i: (h * nq + i, 0, 0))
        st_spec = pl.BlockSpec((1, N_HEADS, HEAD_W, HEAD_W), lambda h, i: (h * nq + i, 0, 0, 0))
        in_specs += [pl.BlockSpec(memory_space=pltpu.SMEM), row_spec, row_spec, row_spec, row_spec, st_spec]
        args += [jnp.exp(_ret_log_decay()), rows(rq), rows(rk), rows(rv), rows(rg), state]
        out_specs += [row_spec, st_spec]
        out_shape += [jax.ShapeDtypeStruct((steps, N_HEADS, HEAD_W), BF16), jax.ShapeDtypeStruct(state.shape, F32)]
    tab_specs = []
    for tab in round_tables:
        assert tab.shape[0] % steps == 0
        tab_specs.append(pl.BlockSpec((tab.shape[0] // steps, tab.shape[1]), lambda h, i: (h * nq + i, 0)))
        args.append(tab)
        out_shape.append(jax.ShapeDtypeStruct(tab.shape, BF16))
    outs = pl.pallas_call(
        functools.partial(_diff_prompt_body, tq=tq, lam_init=lam_init, with_ret=ret_step is not None,
                          n_tab=len(round_tables)),
        grid=(N_HEADS, nq),
        in_specs=in_specs + tab_specs,
        out_specs=out_specs + tab_specs,
        out_shape=out_shape,
        scratch_shapes=[pltpu.VMEM((tq, HEAD_W), F32), pltpu.VMEM((tq, HEAD_W), F32)],
        compiler_params=_cparams(("parallel", "arbitrary")),
        name="diff_prompt",
    )(*args)
    ret_out = None
    if ret_step is not None:
        ret_out = (outs[1].reshape(steps, N_HEADS * HEAD_W), outs[2])
    return outs[0], ret_out, tuple(outs[len(outs) - len(round_tables):])


PAGES_PER_STEP = 8


def _diff_sample_body(pt_ref, q_ref, kn_ref, vn_ref, *rest, page, lam_init):
    ck_refs, cv_refs = rest[:PAGES_PER_STEP], rest[PAGES_PER_STEP:2 * PAGES_PER_STEP]
    lv_ref, g_ref, o_ref, m_sc, l_sc, acc_sc = rest[2 * PAGES_PER_STEP:]
    p = pl.program_id(1)

    @pl.when(p == 0)
    def _():
        m_sc[...] = jnp.full_like(m_sc, -jnp.inf)
        l_sc[...] = jnp.zeros_like(l_sc)
        acc_sc[...] = jnp.zeros_like(acc_sc)

    def fold(n, s):
        m_old = m_sc[n][:, :1]
        m_new = jnp.maximum(m_old, jnp.max(s, axis=-1, keepdims=True))
        alpha = jnp.exp(m_old - m_new)
        pr = jnp.exp(s - m_new)
        l_new = alpha * l_sc[n][:, :1] + jnp.sum(pr, axis=-1, keepdims=True)
        m_sc[n] = jnp.broadcast_to(m_new, m_sc.shape[1:])
        l_sc[n] = jnp.broadcast_to(l_new, l_sc.shape[1:])
        return alpha, pr

    rows = page * N_HEADS
    own_head = (lax.broadcasted_iota(jnp.int32, (N_HEADS, rows), 1) % N_HEADS
                == lax.broadcasted_iota(jnp.int32, (N_HEADS, rows), 0))
    q_both = jnp.concatenate([q_ref[0, 0], q_ref[0, 1]], axis=0).astype(BF16)
    alphas, weights = [], []
    for n in range(2):
        s_pages = []
        for ck_ref in ck_refs:
            k_rows = ck_ref[0, pl.ds(n, rows, stride=2), :].astype(BF16)
            s_pages.append(jnp.where(own_head, _nt(q_both, k_rows)[n * N_HEADS:(n + 1) * N_HEADS], -jnp.inf))
        alpha, pr = fold(n, jnp.concatenate(s_pages, axis=1))
        alphas.append(alpha)
        weights.append(pr)
    w_both = jnp.concatenate(weights, axis=0).astype(BF16)
    pv = jnp.zeros((2 * N_HEADS, HEAD_W), F32)
    for j, cv_ref in enumerate(cv_refs):
        pv += jnp.dot(w_both[:, j * rows:(j + 1) * rows], cv_ref[0].astype(BF16), preferred_element_type=F32)
    for n in range(2):
        acc_sc[n] = alphas[n] * acc_sc[n] + pv[n * N_HEADS:(n + 1) * N_HEADS]

    @pl.when(p == pl.num_programs(1) - 1)
    def _():
        outs = []
        for n in range(2):
            s = jnp.sum(kn_ref[0, n] * q_ref[0, n], axis=-1, keepdims=True)
            alpha, pr = fold(n, s)
            acc_sc[n] = alpha * acc_sc[n] + pr * vn_ref[0]
            outs.append(acc_sc[n] / l_sc[n][:, :1])
        o = outs[0] - _lambda(lv_ref, lam_init) * outs[1]
        o_ref[0] = (_rms(o) * g_ref[...] * (1.0 - lam_init)).astype(o_ref.dtype)


def _diff_sample(dq, dk, dv, cache_k, cache_v, page_table, lam_vecs, subln, lam_init):
    b, npg = page_table.shape
    n_pool, page = cache_k.shape[:2]
    pps = PAGES_PER_STEP
    assert npg % pps == 0
    to_maps = lambda a: a.reshape(b, N_HEADS, 2, DIFF_D).transpose(0, 2, 1, 3)
    ck = cache_k.reshape(n_pool, page * N_HEADS * 2, DIFF_D)
    cv = cache_v.reshape(n_pool, page * N_HEADS, HEAD_W)
    qk_spec = pl.BlockSpec((1, 2, N_HEADS, DIFF_D), lambda i, p, pt: (i, 0, 0, 0))
    hv_spec = pl.BlockSpec((1, N_HEADS, HEAD_W), lambda i, p, pt: (i, 0, 0))

    def page_spec(shape, j):
        return pl.BlockSpec((1,) + shape, lambda i, p, pt: (pt[i * npg + p * pps + j], 0, 0))

    grid_spec = pltpu.PrefetchScalarGridSpec(
        num_scalar_prefetch=1,
        grid=(b, npg // pps),
        in_specs=([qk_spec, qk_spec, hv_spec]
                  + [page_spec((page * N_HEADS * 2, DIFF_D), j) for j in range(pps)]
                  + [page_spec((page * N_HEADS, HEAD_W), j) for j in range(pps)]
                  + [pl.BlockSpec((4, DIFF_D), lambda i, p, pt: (0, 0)),
                     pl.BlockSpec((1, HEAD_W), lambda i, p, pt: (0, 0))]),
        out_specs=hv_spec,
        scratch_shapes=[pltpu.VMEM((2, N_HEADS, LANES), F32), pltpu.VMEM((2, N_HEADS, LANES), F32),
                        pltpu.VMEM((2, N_HEADS, HEAD_W), F32)],
    )
    o = pl.pallas_call(
        functools.partial(_diff_sample_body, page=page, lam_init=lam_init),
        grid_spec=grid_spec,
        out_shape=jax.ShapeDtypeStruct((b, N_HEADS, HEAD_W), BF16),
        compiler_params=_cparams(("parallel", "arbitrary")),
        name="diff_sample",
    )(page_table.reshape(-1), to_maps(dq), to_maps(dk), dv.reshape(b, N_HEADS, HEAD_W),
      *([ck] * pps), *([cv] * pps), lam_vecs, subln.reshape(1, HEAD_W))
    return o.reshape(b, N_HEADS * HEAD_W)


W_PITCH = PEER_NKEYS + 8


def _topk_rows(x, k, ids=None):
    if ids is None:
        ids = lax.broadcasted_iota(jnp.int32, x.shape, 0).astype(F32)
    n = x.shape[0]
    assert n % 8 == 0
    id_slabs = [ids[r:r + 8] for r in range(0, n, 8)]
    vals, idxs = [], []
    for _ in range(k):
        v_l, i_l = [x[r:r + 8] for r in range(0, n, 8)], id_slabs
        while len(v_l) > 1:
            v_n, i_n = [], []
            for a in range(0, len(v_l) - 1, 2):
                first = v_l[a] >= v_l[a + 1]
                v_n.append(jnp.where(first, v_l[a], v_l[a + 1]))
                i_n.append(jnp.where(first, i_l[a], i_l[a + 1]))
            if len(v_l) % 2:
                v_n.append(v_l[-1])
                i_n.append(i_l[-1])
            v_l, i_l = v_n, i_n
        m = jnp.max(v_l[0], axis=0, keepdims=True)
        idx = jnp.min(jnp.where(v_l[0] == m, i_l[0], 1e9), axis=0, keepdims=True)
        x = jnp.where(ids == idx, -jnp.inf, x)
        vals.append(m)
        idxs.append(idx)
    return jnp.concatenate(vals, axis=0), jnp.concatenate(idxs, axis=0)


def _pair_candidates(s1, s2):
    k, t = s1.shape
    assert k == 16
    blocks = [s1[0:1] + s2] + [s1[i:i + 1] + s2[0:8] for i in range(1, 8)] + [s1[8:16] + s2[0:1]]
    r = lax.broadcasted_iota(jnp.int32, (80, t), 0)
    mid = (((r - 16) >> 3) + 1) * k + ((r - 16) & 7)
    ids = jnp.where(r < 16, r, jnp.where(r < 72, mid, (r - 64) * k))
    return jnp.concatenate(blocks, axis=0), ids.astype(F32)


def _pick_rows(table, sel):
    out = jnp.zeros(sel.shape, F32)
    for r in range(table.shape[0]):
        out = jnp.where(sel == float(r), table[r:r + 1], out)
    return out


def _peer_route_body(q_ref, keys_ref, w_ref, a_sc, b_sc, g_sc, wbuf, *, tm):
    qb = q_ref[...].astype(BF16)
    a_rows, b_rows, g_rows = [], [], []
    for h in range(N_HEADS):
        halves = []
        for n in range(2):
            hn = 2 * h + n
            s_t = _nt(keys_ref[hn], qb[:, hn * LANES:(hn + 1) * LANES])
            halves.append(_topk_rows(s_t, PEER_TOPK))
        (s1, i1), (s2, i2) = halves
        cand, cand_ids = _pair_candidates(s1, s2)
        top, sel = _topk_rows(cand, PEER_TOPK, cand_ids)
        sel_i = jnp.floor(sel * (1.0 / PEER_TOPK))
        sel_j = sel - sel_i * PEER_TOPK
        a_rows.append(_pick_rows(i1, sel_i))
        b_rows.append(_pick_rows(i2, sel_j))
        e = jnp.exp(top - top[0:1])
        g_rows.append(e / jnp.sum(e, axis=0, keepdims=True))
    a_sc[...] = jnp.concatenate(a_rows, axis=0).T
    b_sc[...] = jnp.concatenate(b_rows, axis=0).T
    g_sc[...] = jnp.concatenate(g_rows, axis=0).T

    sub = lax.broadcasted_iota(jnp.int32, (PEER_NKEYS, N_HEADS * PEER_TOPK), 0).astype(F32)

    def token(t, carry):
        ar, br, gr = a_sc[pl.ds(t, 1), :], b_sc[pl.ds(t, 1), :], g_sc[pl.ds(t, 1), :]
        p_mat = jnp.where(sub == ar, 1.0, 0.0).astype(BF16)
        q_mat = jnp.where(sub == br, gr, 0.0).astype(BF16)
        wbuf[pl.ds(pl.multiple_of(t * W_PITCH, 8), PEER_NKEYS), :] = _nt(p_mat, q_mat)
        return carry

    lax.fori_loop(0, tm, token, 0, unroll=64)
    for a in range(PEER_NKEYS):
        w_ref[a] = wbuf[pl.ds(a, tm, stride=W_PITCH), :].astype(w_ref.dtype)


def _peer_route(q, keys, tm=128):
    m = q.shape[0]
    tm = min(tm, m)
    slots = N_HEADS * PEER_TOPK
    return pl.pallas_call(
        functools.partial(_peer_route_body, tm=tm),
        grid=(m // tm,),
        in_specs=[pl.BlockSpec((tm, q.shape[1]), lambda i: (i, 0)),
                  pl.BlockSpec(keys.shape, lambda i: (0, 0, 0))],
        out_specs=pl.BlockSpec((PEER_NKEYS, tm, PEER_NKEYS), lambda i: (0, i, 0)),
        out_shape=jax.ShapeDtypeStruct((PEER_NKEYS, m, PEER_NKEYS), BF16),
        scratch_shapes=[pltpu.VMEM((tm, slots), F32), pltpu.VMEM((tm, slots), F32), pltpu.VMEM((tm, slots), F32),
                        pltpu.VMEM((tm * W_PITCH, PEER_NKEYS), F32)],
        compiler_params=_cparams(("parallel",)),
        name="peer_route",
    )(q, keys)


def _peer_dense_body(h_ref, u_ref, v_ref, w_ref, *rest, ta, td):
    has_small = len(rest) == 4
    o_ref = rest[2] if has_small else rest[0]
    i, n = pl.program_id(0), pl.program_id(1)

    def expert_tile(h, w_blk, out):
        s = _nt(h, u_ref[...])
        act = 0.5 * s * (1.0 + lax.erf(s * INV_SQRT2))
        c = jnp.concatenate([act[:, a * LANES:(a + 1) * LANES] * w_blk[a].astype(F32) for a in range(ta)], axis=1)
        c = c.astype(BF16)
        for d0 in range(0, out.shape[1], td):
            out[:, d0:d0 + td] += jnp.dot(c, v_ref[:, d0:d0 + td], preferred_element_type=F32)

    @pl.when(n == 0)
    def _():
        o_ref[...] = jnp.zeros_like(o_ref)

    if has_small:
        hs_ref, ws_ref, _, os_ref = rest

        @pl.when(jnp.logical_and(i == 0, n == 0))
        def _():
            os_ref[...] = jnp.zeros_like(os_ref)

        @pl.when(i == 0)
        def _():
            expert_tile(hs_ref[...], ws_ref, os_ref)

    expert_tile(h_ref[...], w_ref, o_ref)


def _peer_dense(hn, u, v, w, tm=512, tn=1024, td=1024, small=None):
    m, d = hn.shape
    n_exp = v.shape[0]
    tm = min(tm, m)
    ta = tn // LANES
    once = pl.Buffered(1)
    in_specs = [pl.BlockSpec((tm, d), lambda i, n: (i, 0), pipeline_mode=once),
                pl.BlockSpec((tn, d), lambda i, n: (n, 0)),
                pl.BlockSpec((tn, d), lambda i, n: (n, 0)),
                pl.BlockSpec((ta, tm, LANES), lambda i, n: (n, i, 0))]
    args = [hn, u, v, w]
    out_specs = [pl.BlockSpec((tm, d), lambda i, n: (i, 0), pipeline_mode=once)]
    out_shape = [jax.ShapeDtypeStruct((m, d), F32)]
    if small is not None:
        hn2, w2 = small
        m2 = hn2.shape[0]
        in_specs += [pl.BlockSpec((m2, d), lambda i, n: (0, 0), pipeline_mode=once),
                     pl.BlockSpec((ta, m2, LANES), lambda i, n: (n, 0, 0))]
        args += [hn2, w2]
        out_specs.append(pl.BlockSpec((m2, d), lambda i, n: (0, 0), pipeline_mode=once))
        out_shape.append(jax.ShapeDtypeStruct((m2, d), F32))
    outs = pl.pallas_call(
        functools.partial(_peer_dense_body, ta=ta, td=td),
        grid=(m // tm, n_exp // tn),
        in_specs=in_specs,
        out_specs=out_specs,
        out_shape=out_shape,
        compiler_params=_cparams(("parallel" if small is None else "arbitrary", "arbitrary")),
        name="peer_dense",
    )(*args)
    return outs if small is not None else outs[0]


def _rope_tables(pos, d, tiled=False):
    inv = ROPE_THETA ** (-jnp.arange(0, d, 2, dtype=F32) / d)
    if tiled:
        inv = jnp.concatenate([inv, inv])
    ang = pos.astype(F32)[:, None] * inv[None, :]
    if tiled:
        sign = jnp.concatenate([-jnp.ones((d // 2,), F32), jnp.ones((d // 2,), F32)])
        return jnp.cos(ang), jnp.sin(ang) * sign[None, :]
    return jnp.cos(ang), jnp.sin(ang)


def _in_proj(xp, xs, pos_p, pos_s, norm_mix, w_in):
    seg = N_HEADS * HEAD_W
    xn_p, xn_s = _rmsnorm(xp, norm_mix, BF16), _rmsnorm(xs, norm_mix, BF16)

    def tables(pos):
        c256, s256 = _rope_tables(pos, HEAD_W)
        cc, ss = _rope_tables(pos, DIFF_D, tiled=True)
        return ((c256, "row"), (s256, "row")), ((cc, "row"), (ss, "row"))

    (rope_r_p, rope_d_p), (rope_r_s, rope_d_s) = tables(pos_p), tables(pos_s)

    def seg_proj(i, n, epi, dts_p, dts_s, extras=((), ()), scales=(1.0, 1.0)):
        outs = _proj(xn_p, w_in, i * seg, n * seg, epi, dts_p, extras=extras[0], scale=scales[0],
                     name=f"in_proj_{i}", small=(xn_s, extras[1], scales[1], dts_s))
        return outs[:len(dts_p)], outs[len(dts_p):]

    pp, ps = {}, {}
    (pp["rq"],), (ps["rq"],) = seg_proj(0, 1, "rope256", [BF16], [F32], (rope_r_p, rope_r_s))
    (pp["rk"],), (ps["rk"],) = seg_proj(1, 1, "rope256", [BF16], [F32], (rope_r_p, rope_r_s),
                                        (HEAD_W ** -0.5, HEAD_W ** -0.5))
    (pp["rvg"],), (ps["rvg"],) = seg_proj(2, 2, "plain", [BF16], [F32])
    (pp["dq"],), (ps["dq"],) = seg_proj(4, 1, "rope128", [BF16], [F32], (rope_d_p, rope_d_s),
                                        (DIFF_D ** -0.5 * LOG2E, DIFF_D ** -0.5))
    (pp["dk_f"], pp["dk_b"]), (ps["dk_f"],) = seg_proj(5, 1, "rope128", [F32, BF16], [F32], (rope_d_p, rope_d_s))
    (pp["dv_f"], pp["dv_b"]), (ps["dv_f"],) = seg_proj(6, 1, "plain", [F32, BF16], [F32])
    (pp["gates"],), (ps["gates"],) = seg_proj(7, 4, "sigmoid", [BF16], [BF16])
    return pp, ps


def _tail(xp, xs, pp, ps, mix_p, mix_s, w_ret_out, w_diff_out, w_o, norm_ffn, peer_w_q, peer_keys, peer_u, peer_v,
          norm_final):
    d = xp.shape[1]
    merged_p, merged_s = _merge(*mix_p, w_ret_out, w_diff_out, pp["gates"], small=(*mix_s, ps["gates"]))
    x1_p, x1_s = _proj(merged_p, w_o, 0, d, "residual", [F32], extras=((xp, "tile"),), name="out_proj",
                       small=(merged_s, ((xs, "tile"),), 1.0, [F32]))
    hn_p, hn_s = _rmsnorm(x1_p, norm_ffn, BF16), _rmsnorm(x1_s, norm_ffn, BF16)
    pq_p, pq_s = _proj(hn_p, peer_w_q, 0, peer_w_q.shape[1], "plain", [F32], name="peer_q",
                       small=(hn_s, (), 1.0, [F32]))
    w_p, w_s = _peer_route(pq_p, peer_keys), _peer_route(pq_s, peer_keys)
    d_p, d_s = _peer_dense(hn_p, peer_u, peer_v, w_p, small=(hn_s, w_s))
    return _add_rmsnorm(x1_p, d_p, norm_final), _add_rmsnorm(x1_s, d_s, norm_final)


def kernel(x_prompt, x_sample, cache_k, cache_v, state_ret, page_table, norm_mix, w_in, lambda_q1, lambda_k1,
           lambda_q2, lambda_k2, diff_subln, w_ret_out, w_diff_out, w_o, norm_ffn, peer_w_q, peer_keys, peer_u,
           peer_v, norm_final):
    depth = w_in.shape[0]
    assert depth == 1
    b_p, t_p, d = x_prompt.shape
    b_s, t_s, _ = x_sample.shape
    assert b_p == 1 and t_s == 1
    past = page_table.shape[1] * cache_k.shape[2]
    seg = N_HEADS * HEAD_W
    l = 0
    lam_init = 0.8 - 0.6 * math.exp(-0.3 * l)
    lam_vecs = jnp.stack([lambda_q1[l], lambda_k1[l], lambda_q2[l], lambda_k2[l]])
    keys = peer_keys[l].reshape(N_HEADS * 2, PEER_NKEYS, -1).astype(BF16)
    xp = x_prompt.reshape(t_p, d)
    xs = x_sample.reshape(b_s, d)

    pp, ps = _in_proj(xp, xs, jnp.arange(t_p), jnp.full((b_s,), past), norm_mix[l], w_in[l])

    ret_o_p, state_p = _ret_prompt(pp["rq"], pp["rk"], pp["rvg"])
    diff_o_p, (ret_o_s, state_s), (u_bf, v_bf) = _diff_prompt(
        pp["dq"], pp["dk_b"], pp["dv_b"], lam_vecs, diff_subln[l], lam_init,
        ret_step=(ps["rq"], ps["rk"], ps["rvg"][:, :seg], ps["rvg"][:, seg:], state_ret[l]),
        round_tables=(peer_u[l], peer_v[l]))
    diff_o_s = _diff_sample(ps["dq"], ps["dk_f"], ps["dv_f"], cache_k[l], cache_v[l], page_table, lam_vecs,
                            diff_subln[l], lam_init)

    y_p, y_s = _tail(xp, xs, pp, ps, (ret_o_p, diff_o_p), (ret_o_s, diff_o_s), w_ret_out[l], w_diff_out[l], w_o[l],
                     norm_ffn[l], peer_w_q[l], keys, u_bf, v_bf, norm_final)
    k_prompt = pp["dk_f"].reshape(1, b_p, t_p, N_HEADS, 2, DIFF_D)
    v_prompt = pp["dv_f"].reshape(1, b_p, t_p, N_HEADS, HEAD_W)
    k_sample = ps["dk_f"].reshape(1, b_s, t_s, N_HEADS, 2, DIFF_D)
    v_sample = ps["dv_f"].reshape(1, b_s, t_s, N_HEADS, HEAD_W)
    return (y_p.reshape(b_p, t_p, d), y_s.reshape(b_s, t_s, d), k_prompt, v_prompt, state_p[None, None],
            k_sample, v_sample, state_s[None])
```
